```python
import math
import jax, jax.numpy as jnp
from jax import lax
import numpy as np

D_MODEL = 1024
BATCH = 4
SEQ = 4096
DEPTH = 4
DEC_BATCH = 32
DEC_SEQ = 8
PAST_LEN = 8192
PAGE_SIZE = 128

D_FF = 2816
D_CONV_A = D_MODEL // 2
CONV_A_WIDTH = 31
N_HEADS = 16
HEAD_DIM = 64
D_ATTN = N_HEADS * HEAD_DIM
D_SHORT = D_MODEL // 2
SHORT_WIDTH = 3
N_BRANCH = 3
Q_BLOCK = 128
EPS = 1e-6
N_IN = 2 * D_CONV_A + 3 * D_ATTN + 3 * D_SHORT + N_BRANCH * D_MODEL

kernel_name = "macaron_gated_conformer_stickbreak_shortconv_step"


def _rmsnorm(x, g):
    xf = x.astype(jnp.float32)
    y = xf * lax.rsqrt(jnp.mean(xf * xf, axis=-1, keepdims=True) + EPS)
    return (y * g.astype(jnp.float32)).astype(x.dtype)


def _layernorm(x, g, b):
    xf = x.astype(jnp.float32)
    mu = jnp.mean(xf, axis=-1, keepdims=True)
    var = jnp.mean(jnp.square(xf - mu), axis=-1, keepdims=True)
    y = (xf - mu) * lax.rsqrt(var + EPS) * g.astype(jnp.float32) + b.astype(jnp.float32)
    return y.astype(x.dtype)


def _swiglu(x, wg, wu, wd):
    return (jax.nn.silu(x @ wg) * (x @ wu)) @ wd


def _causal_dwconv(x, buf, w):
    xp = jnp.concatenate([buf.astype(x.dtype), x], axis=1)
    y = lax.conv_general_dilated(xp, w[:, None, :].astype(x.dtype), window_strides=(1,), padding="VALID",
                                 dimension_numbers=("NWC", "WIO", "NWC"), feature_group_count=x.shape[-1])
    return y, xp[:, -(w.shape[0] - 1):]


def _stick_breaking(q, k, v, q_pos, k_pos, bias):
    z = jnp.einsum("bqhd,bkhd->bhqk", q.astype(jnp.float32), k.astype(jnp.float32)) * (HEAD_DIM ** -0.5)
    z = z + bias.astype(jnp.float32)[None, :, None, None]
    mask = (k_pos[None, :] < q_pos[:, None])[None, None]
    log_keep = jnp.where(mask, jax.nn.log_sigmoid(-z), 0.0)
    later = lax.cumsum(log_keep, axis=3, reverse=True) - log_keep
    a = jnp.where(mask, jnp.exp(jax.nn.log_sigmoid(z) + later), 0.0)
    return jnp.einsum("bhqk,bkhd->bqhd", a, v.astype(jnp.float32)).astype(v.dtype)


def _attend(q, k, v, q_start, bias):
    b, tq, h, hd = q.shape
    k_pos = jnp.arange(k.shape[1])
    q_pos = q_start + jnp.arange(tq)
    if tq > Q_BLOCK and tq % Q_BLOCK == 0:
        nb = tq // Q_BLOCK
        qb = q.reshape(b, nb, Q_BLOCK, h, hd).transpose(1, 0, 2, 3, 4)
        out = lax.map(lambda qp: _stick_breaking(qp[0], k, v, qp[1], k_pos, bias),
                      (qb, q_pos.reshape(nb, Q_BLOCK)))
        return out.transpose(1, 0, 2, 3, 4).reshape(b, tq, h, hd)
    return _stick_breaking(q, k, v, q_pos, k_pos, bias)


def _token_mixer(u, buf_a, buf_c, k_past, v_past, w_in, conv_a_w, conv_a_b, ln_a_g, ln_a_b,
                 w_a_out, w_b_out, sb_bias, conv_c_w, w_c_out, w_o):
    b, t, _ = u.shape
    sizes = [D_CONV_A, D_CONV_A, D_ATTN, D_ATTN, D_ATTN, D_SHORT, D_SHORT, D_SHORT]
    splits = list(np.cumsum(sizes))
    a_val, a_gate, q, k, v, c_in, c_b, c_c, gates = jnp.split(u @ w_in, splits, axis=-1)
    a = a_val * jax.nn.sigmoid(a_gate)
    a_conv, new_buf_a = _causal_dwconv(a, buf_a, conv_a_w)
    branch_a = jax.nn.silu(_layernorm(a_conv + conv_a_b, ln_a_g, ln_a_b)) @ w_a_out
    q = q.reshape(b, t, N_HEADS, HEAD_DIM)
    k = k.reshape(b, t, N_HEADS, HEAD_DIM)
    v = v.reshape(b, t, N_HEADS, HEAD_DIM)
    k_all = jnp.concatenate([k_past.astype(k.dtype), k], axis=1)
    v_all = jnp.concatenate([v_past.astype(v.dtype), v], axis=1)
    o = _attend(q, k_all, v_all, k_past.shape[1], sb_bias)
    branch_b = o.reshape(b, t, D_ATTN) @ w_b_out
    c_conv, new_buf_c = _causal_dwconv(c_c * c_in, buf_c, conv_c_w)
    branch_c = (c_b * c_conv) @ w_c_out
    g = jax.nn.sigmoid(gates).reshape(b, t, N_BRANCH, D_MODEL)
    merged = g[:, :, 0] * branch_a + g[:, :, 1] * branch_b + g[:, :, 2] * branch_c
    return merged @ w_o, new_buf_a, new_buf_c, k, v


def _layer(x, buf_a, buf_c, k_past, v_past,
           f1_pre, f1_post, f1_wg, f1_wu, f1_wd, m_pre, m_post, w_in, conv_a_w, conv_a_b, ln_a_g, ln_a_b,
           w_a_out, w_b_out, sb_bias, conv_c_w, w_c_out, w_o, f2_pre, f2_post, f2_wg, f2_wu, f2_wd):
    x = x + 0.5 * _rmsnorm(_swiglu(_rmsnorm(x, f1_pre), f1_wg, f1_wu, f1_wd), f1_post)
    mix, nba, nbc, k, v = _token_mixer(_rmsnorm(x, m_pre), buf_a, buf_c, k_past, v_past, w_in, conv_a_w,
                                       conv_a_b, ln_a_g, ln_a_b, w_a_out, w_b_out, sb_bias, conv_c_w,
                                       w_c_out, w_o)
    x = x + _rmsnorm(mix, m_post)
    x = x + 0.5 * _rmsnorm(_swiglu(_rmsnorm(x, f2_pre), f2_wg, f2_wu, f2_wd), f2_post)
    return x, nba, nbc, k, v


def setup_inputs(seed: int = 0) -> dict:
    key = jax.random.key(seed)
    ks = iter(jax.random.split(key, 40))

    def nrm(shape, scale=1.0):
        return jax.random.normal(next(ks), shape, jnp.float32) * scale

    def gain(shape):
        return 1.0 + 0.05 * nrm(shape)

    n_pages = PAST_LEN // PAGE_SIZE
    n_used = DEC_BATCH * n_pages
    n_pool = n_used + max(1, n_used // 4)
    page_table = jax.random.permutation(next(ks), n_pool)[:n_used].reshape(DEC_BATCH, n_pages).astype(jnp.int32)
    L, D = DEPTH, D_MODEL
    sb_bias0 = -math.log(PAST_LEN + DEC_SEQ)
    return {
        "x_prompt": nrm((BATCH, SEQ, D)),
        "x_sample": nrm((DEC_BATCH, DEC_SEQ, D)),
        "cache_k": nrm((L, n_pool, PAGE_SIZE, N_HEADS, HEAD_DIM)),
        "cache_v": nrm((L, n_pool, PAGE_SIZE, N_HEADS, HEAD_DIM)),
        "state_conv_a": nrm((L, DEC_BATCH, CONV_A_WIDTH - 1, D_CONV_A), 0.5),
        "state_conv_c": nrm((L, DEC_BATCH, SHORT_WIDTH - 1, D_SHORT), 0.5),
        "page_table": page_table,
        "ffn1_pre_g": gain((L, D)), "ffn1_post_g": gain((L, D)),
        "ffn1_w_gate": nrm((L, D, D_FF), D ** -0.5), "ffn1_w_up": nrm((L, D, D_FF), D ** -0.5),
        "ffn1_w_down": nrm((L, D_FF, D), D_FF ** -0.5),
        "mix_pre_g": gain((L, D)), "mix_post_g": gain((L, D)),
        "w_in": nrm((L, D, N_IN), D ** -0.5),
        "conv_a_w": nrm((L, CONV_A_WIDTH, D_CONV_A), CONV_A_WIDTH ** -0.5),
        "conv_a_b": nrm((L, D_CONV_A), 0.02),
        "ln_a_g": gain((L, D_CONV_A)), "ln_a_b": nrm((L, D_CONV_A), 0.02),
        "w_a_out": nrm((L, D_CONV_A, D), D_CONV_A ** -0.5),
        "w_b_out": nrm((L, D_ATTN, D), D_ATTN ** -0.5),
        "sb_bias": sb_bias0 + 0.1 * nrm((L, N_HEADS)),
        "conv_c_w": nrm((L, SHORT_WIDTH, D_SHORT), SHORT_WIDTH ** -0.5),
        "w_c_out": nrm((L, D_SHORT, D), D_SHORT ** -0.5),
        "w_o": nrm((L, D, D), D ** -0.5),
        "ffn2_pre_g": gain((L, D)), "ffn2_post_g": gain((L, D)),
        "ffn2_w_gate": nrm((L, D, D_FF), D ** -0.5), "ffn2_w_up": nrm((L, D, D_FF), D ** -0.5),
        "ffn2_w_down": nrm((L, D_FF, D), D_FF ** -0.5),
    }


def reference(x_prompt, x_sample, cache_k, cache_v, state_conv_a, state_conv_c, page_table,
              ffn1_pre_g, ffn1_post_g, ffn1_w_gate, ffn1_w_up, ffn1_w_down, mix_pre_g, mix_post_g, w_in,
              conv_a_w, conv_a_b, ln_a_g, ln_a_b, w_a_out, w_b_out, sb_bias, conv_c_w, w_c_out, w_o,
              ffn2_pre_g, ffn2_post_g, ffn2_w_gate, ffn2_w_up, ffn2_w_down):
    dt = x_prompt.dtype
    n_sample, n_pages = page_table.shape
    past_len = n_pages * cache_k.shape[2]
    xp, xs = x_prompt, x_sample
    zero_a = jnp.zeros((xp.shape[0], CONV_A_WIDTH - 1, D_CONV_A), dt)
    zero_c = jnp.zeros((xp.shape[0], SHORT_WIDTH - 1, D_SHORT), dt)
    empty_kv = jnp.zeros((xp.shape[0], 0, N_HEADS, HEAD_DIM), dt)
    kp_l, vp_l, ap_l, cp_l, ks_l, vs_l, as_l, cs_l = [], [], [], [], [], [], [], []
    for l in range(DEPTH):
        lw = (ffn1_pre_g[l], ffn1_post_g[l], ffn1_w_gate[l], ffn1_w_up[l], ffn1_w_down[l], mix_pre_g[l],
              mix_post_g[l], w_in[l], conv_a_w[l], conv_a_b[l], ln_a_g[l], ln_a_b[l], w_a_out[l], w_b_out[l],
              sb_bias[l], conv_c_w[l], w_c_out[l], w_o[l], ffn2_pre_g[l], ffn2_post_g[l], ffn2_w_gate[l],
              ffn2_w_up[l], ffn2_w_down[l])
        xp, ba, bc, k_new, v_new = _layer(xp, zero_a, zero_c, empty_kv, empty_kv, *lw)
        kp_l.append(k_new); vp_l.append(v_new); ap_l.append(ba); cp_l.append(bc)
        k_past = jnp.take(cache_k[l], page_table, axis=0).reshape(n_sample, past_len, N_HEADS, HEAD_DIM)
        v_past = jnp.take(cache_v[l], page_table, axis=0).reshape(n_sample, past_len, N_HEADS, HEAD_DIM)
        xs, ba, bc, k_new, v_new = _layer(xs, state_conv_a[l], state_conv_c[l], k_past, v_past, *lw)
        ks_l.append(k_new); vs_l.append(v_new); as_l.append(ba); cs_l.append(bc)
    return (xp, xs, jnp.stack(kp_l), jnp.stack(vp_l), jnp.stack(ap_l), jnp.stack(cp_l),
            jnp.stack(ks_l), jnp.stack(vs_l), jnp.stack(as_l), jnp.stack(cs_l))
```

```python
import functools

import jax
import jax.numpy as jnp
from jax import lax
from jax.experimental import pallas as pl
from jax.experimental.pallas import tpu as pltpu

EPS = 1e-6
N_HEADS = 16
HEAD_DIM = 64
D_ATTN = N_HEADS * HEAD_DIM
LANES = 128
SUBLANES = 8
HEADS_PER_GROUP = LANES // HEAD_DIM
VMEM_LIMIT = 56 * 1024 * 1024

F32 = jnp.float32
BF16 = jnp.bfloat16


def _params(n_grid):
    return pltpu.CompilerParams(dimension_semantics=("arbitrary",) * n_grid,
                                vmem_limit_bytes=VMEM_LIMIT)


def _resident(shape):
    zeros = (0,) * len(shape)
    return pl.BlockSpec(shape, lambda *_: zeros, pipeline_mode=pl.Buffered(1))


def _rms(x, g):
    return x * lax.rsqrt(jnp.mean(x * x, axis=-1, keepdims=True) + EPS) * g


def _silu(x):
    return x * jax.nn.sigmoid(x)


def _softplus(z):
    return jnp.maximum(z, 0.0) + jnp.log1p(jnp.exp(-jnp.abs(z)))


def _dot(a, b):
    return jnp.dot(a, b, preferred_element_type=F32)


def _dot_nt(a, b):
    return lax.dot_general(a, b, (((1,), (1,)), ((), ())), preferred_element_type=F32)


def _ffn_body(x_ref, pre_ref, post_ref, wg_ref, wu_ref, wd_ref, o_ref):
    x = x_ref[...]
    xn = _rms(x, pre_ref[...]).astype(BF16)
    h = (_silu(_dot(xn, wg_ref[...])) * _dot(xn, wu_ref[...])).astype(BF16)
    y = _dot(h, wd_ref[...])
    o_ref[...] = x + 0.5 * _rms(y, post_ref[...])


def _ffn(x, pre_g, post_g, wg, wu, wd, tile):
    n, d = x.shape
    f = wg.shape[1]
    row = pl.BlockSpec((tile, d), lambda i: (i, 0))
    return pl.pallas_call(
        _ffn_body,
        grid=(n // tile,),
        in_specs=[row, _resident((1, d)), _resident((1, d)),
                  _resident((d, f)), _resident((d, f)), _resident((f, d))],
        out_specs=row,
        out_shape=jax.ShapeDtypeStruct((n, d), F32),
        compiler_params=_params(1),
        name="ffn",
    )(x, pre_g, post_g, wg, wu, wd)


def _mix_columns(d_a, d_c):
    sizes = [d_a, d_a, D_ATTN, D_ATTN, D_ATTN, d_c, d_c, d_c]
    offs = [0]
    for s in sizes:
        offs.append(offs[-1] + s)
    return offs


def _layernorm_silu(t, g, b):
    mu = jnp.mean(t, axis=-1, keepdims=True)
    c = t - mu
    var = jnp.mean(c * c, axis=-1, keepdims=True)
    return _silu(c * lax.rsqrt(var + EPS) * g + b)


def _proj_common(x, pre_ref, w_ref, offs, q_ref, k_ref, v_ref):
    u = _rms(x, pre_ref[...]).astype(BF16)

    def col(i):
        return _dot(u, w_ref[:, offs[i]:offs[i + 1]])

    a = col(0) * jax.nn.sigmoid(col(1))
    q_ref[...] = (col(2) * (HEAD_DIM ** -0.5)).astype(q_ref.dtype)
    k = col(3)
    v = col(4)
    k_ref[...] = k
    v_ref[...] = v
    c_in = col(5)
    c_b = col(6)
    c_c = col(7)
    return a, c_c * c_in, c_b, k, v


def _mixproj_prompt_body(x_ref, pre_ref, w_ref, sta_ref, stc_ref, caw_ref, cab_ref, lng_ref, lnb_ref,
                         ccw_ref, q_ref, k_ref, v_ref, kb_ref, vb_ref, sa_ref, cp_ref, na_ref, nc_ref,
                         xa_scr, xc_scr, *, tile, offs, wa, wc, pad_a, pad_c):
    i = pl.program_id(1)
    halo_a = wa - 1
    halo_c = wc - 1

    @pl.when(i == 0)
    def _():
        xa_scr[0:pad_a, :] = sta_ref[...]
        xc_scr[0:pad_c, :] = stc_ref[...]

    @pl.when(i > 0)
    def _():
        xa_scr[pad_a - halo_a:pad_a, :] = xa_scr[tile + pad_a - halo_a:tile + pad_a, :]
        xc_scr[pad_c - halo_c:pad_c, :] = xc_scr[tile + pad_c - halo_c:tile + pad_c, :]

    a, cc, c_b, k, v = _proj_common(x_ref[...], pre_ref, w_ref, offs, q_ref, k_ref, v_ref)
    kb_ref[...] = k.astype(BF16)
    vb_ref[...] = v.astype(BF16)
    xa_scr[pad_a:pad_a + tile, :] = a
    xc_scr[pad_c:pad_c + tile, :] = cc

    conv = cab_ref[...] + caw_ref[0:1, :] * xa_scr[pad_a - halo_a:pad_a - halo_a + tile, :]
    for j in range(1, wa):
        s = pad_a - halo_a + j
        conv = conv + caw_ref[j:j + 1, :] * xa_scr[s:s + tile, :]
    sa_ref[...] = _layernorm_silu(conv, lng_ref[...], lnb_ref[...]).astype(BF16)

    cconv = ccw_ref[0:1, :] * xc_scr[pad_c - halo_c:pad_c - halo_c + tile, :]
    for j in range(1, wc):
        s = pad_c - halo_c + j
        cconv = cconv + ccw_ref[j:j + 1, :] * xc_scr[s:s + tile, :]
    cp_ref[...] = (c_b * cconv).astype(BF16)

    @pl.when(i == pl.num_programs(1) - 1)
    def _():
        na_ref[...] = xa_scr[tile + pad_a - halo_a:tile + pad_a, :]
        nc_ref[...] = xc_scr[tile + pad_c - halo_c:tile + pad_c, :]


def _mixproj_prompt(x, pre_g, w_main, sta_pad, stc_pad, caw, cab, lng, lnb, ccw, tile):
    b, t, d = x.shape
    wa, d_a = caw.shape
    wc, d_c = ccw.shape
    pad_a, pad_c = sta_pad.shape[1], stc_pad.shape[1]
    offs = _mix_columns(d_a, d_c)
    rows = lambda w: pl.BlockSpec((None, tile, w), lambda bi, i: (bi, i, 0))
    per_seq = lambda r, w: pl.BlockSpec((None, r, w), lambda bi, i: (bi, 0, 0))
    body = functools.partial(_mixproj_prompt_body, tile=tile, offs=offs, wa=wa, wc=wc,
                             pad_a=pad_a, pad_c=pad_c)
    return pl.pallas_call(
        body,
        grid=(b, t // tile),
        in_specs=[rows(d), _resident((1, d)), _resident(w_main.shape),
                  per_seq(pad_a, d_a), per_seq(pad_c, d_c),
                  _resident(caw.shape), _resident((1, d_a)), _resident((1, d_a)), _resident((1, d_a)),
                  _resident(ccw.shape)],
        out_specs=[rows(D_ATTN), rows(D_ATTN), rows(D_ATTN), rows(D_ATTN), rows(D_ATTN),
                   rows(d_a), rows(d_c), per_seq(wa - 1, d_a), per_seq(wc - 1, d_c)],
        out_shape=[jax.ShapeDtypeStruct((b, t, D_ATTN), BF16),
                   jax.ShapeDtypeStruct((b, t, D_ATTN), F32),
                   jax.ShapeDtypeStruct((b, t, D_ATTN), F32),
                   jax.ShapeDtypeStruct((b, t, D_ATTN), BF16),
                   jax.ShapeDtypeStruct((b, t, D_ATTN), BF16),
                   jax.ShapeDtypeStruct((b, t, d_a), BF16),
                   jax.ShapeDtypeStruct((b, t, d_c), BF16),
                   jax.ShapeDtypeStruct((b, wa - 1, d_a), F32),
                   jax.ShapeDtypeStruct((b, wc - 1, d_c), F32)],
        scratch_shapes=[pltpu.VMEM((pad_a + tile, d_a), F32), pltpu.VMEM((pad_c + tile, d_c), F32)],
        compiler_params=_params(2),
        name="mixproj_prompt",
    )(x, pre_g, w_main, sta_pad, stc_pad, caw, cab, lng, lnb, ccw)


def _mixproj_sample_body(x_ref, pre_ref, w_ref, sta_ref, stc_ref, caw_ref, cab_ref, lng_ref, lnb_ref,
                         ccw_ref, q_ref, k_ref, v_ref, sa_ref, cp_ref, na_ref, nc_ref,
                         xa_scr, xc_scr, *, n_seq, t_seq, offs, wa, wc, pad_a, pad_c):
    halo_a = wa - 1
    halo_c = wc - 1
    a, cc, c_b, _, _ = _proj_common(x_ref[...], pre_ref, w_ref, offs, q_ref, k_ref, v_ref)
    d_a = a.shape[-1]
    d_c = cc.shape[-1]
    xa_scr[:, 0:pad_a, :] = sta_ref[...]
    xc_scr[:, 0:pad_c, :] = stc_ref[...]
    xa_scr[:, pad_a:pad_a + t_seq, :] = a.reshape(n_seq, t_seq, d_a)
    xc_scr[:, pad_c:pad_c + t_seq, :] = cc.reshape(n_seq, t_seq, d_c)

    conv = cab_ref[...] + caw_ref[0:1, :] * xa_scr[:, pad_a - halo_a:pad_a - halo_a + t_seq, :]
    for j in range(1, wa):
        s = pad_a - halo_a + j
        conv = conv + caw_ref[j:j + 1, :] * xa_scr[:, s:s + t_seq, :]
    sa = _layernorm_silu(conv, lng_ref[...], lnb_ref[...])
    sa_ref[...] = sa.reshape(n_seq * t_seq, d_a).astype(BF16)

    cconv = ccw_ref[0:1, :] * xc_scr[:, pad_c - halo_c:pad_c - halo_c + t_seq, :]
    for j in range(1, wc):
        s = pad_c - halo_c + j
        cconv = cconv + ccw_ref[j:j + 1, :] * xc_scr[:, s:s + t_seq, :]
    cp_ref[...] = (c_b * cconv.reshape(n_seq * t_seq, d_c)).astype(BF16)

    na_ref[...] = xa_scr[:, pad_a + t_seq - halo_a:pad_a + t_seq, :]
    nc_ref[...] = xc_scr[:, pad_c + t_seq - halo_c:pad_c + t_seq, :]


def _mixproj_sample(x, pre_g, w_main, sta_pad, stc_pad, caw, cab, lng, lnb, ccw, n_seq, t_seq):
    n, d = x.shape
    wa, d_a = caw.shape
    wc, d_c = ccw.shape
    pad_a, pad_c = sta_pad.shape[1], stc_pad.shape[1]
    offs = _mix_columns(d_a, d_c)
    body = functools.partial(_mixproj_sample_body, n_seq=n_seq, t_seq=t_seq, offs=offs, wa=wa, wc=wc,
                             pad_a=pad_a, pad_c=pad_c)
    full = lambda a: _resident(a.shape)
    out_shape = [jax.ShapeDtypeStruct((n, D_ATTN), F32),
                 jax.ShapeDtypeStruct((n, D_ATTN), F32),
                 jax.ShapeDtypeStruct((n, D_ATTN), F32),
                 jax.ShapeDtypeStruct((n, d_a), BF16),
                 jax.ShapeDtypeStruct((n, d_c), BF16),
                 jax.ShapeDtypeStruct((n_seq, wa - 1, d_a), F32),
                 jax.ShapeDtypeStruct((n_seq, wc - 1, d_c), F32)]
    return pl.pallas_call(
        body,
        grid=(1,),
        in_specs=[full(x), full(pre_g), full(w_main), full(sta_pad), full(stc_pad),
                  full(caw), full(cab), full(lng), full(lnb), full(ccw)],
        out_specs=[pl.BlockSpec(s.shape, lambda i, nd=len(s.shape): (0,) * nd) for s in out_shape],
        out_shape=out_shape,
        scratch_shapes=[pltpu.VMEM((n_seq, pad_a + t_seq, d_a), F32),
                        pltpu.VMEM((n_seq, pad_c + t_seq, d_c), F32)],
        compiler_params=_params(1),
        name="mixproj_sample",
    )(x, pre_g, w_main, sta_pad, stc_pad, caw, cab, lng, lnb, ccw)


def _suffix_sum_matrix(n):
    j = jnp.arange(n)[:, None]
    s = jnp.arange(n)[None, :]
    return jnp.concatenate([(j >= s).astype(BF16), jnp.ones((n, LANES), BF16)], axis=1)


def _stick_tile(z, carry, u, vt, mask):
    n = z.shape[1]
    sp = _softplus(z)
    if mask is not None:
        sp = jnp.where(mask, sp, 0.0)
    r = _dot(sp.astype(BF16), u)
    incl = r[:, :n]
    tot = r[:, n:]
    cb = carry if n == LANES else jnp.concatenate([carry] * (n // LANES), axis=1)
    a = jnp.exp(z - incl - cb)
    if mask is not None:
        a = jnp.where(mask, a, 0.0)
    return _dot(a.astype(BF16), vt), carry + tot


def _attn_prompt_body(bias_ref, q_ref, k_ref, v_ref, u_ref, o_ref, *, tq):
    hp = pl.program_id(1)
    i = pl.program_id(2)
    q = q_ref[...]
    lane = lax.broadcasted_iota(jnp.int32, (tq, LANES), 1)
    zero = jnp.zeros_like(q)
    qh = [jnp.where((lane // HEAD_DIM) == h, q, zero) for h in range(HEADS_PER_GROUP)]
    bh = [bias_ref[hp * HEADS_PER_GROUP + h] for h in range(HEADS_PER_GROUP)]
    u = u_ref[...]
    causal = (lax.broadcasted_iota(jnp.int32, (tq, tq), 1) < lax.broadcasted_iota(jnp.int32, (tq, tq), 0))

    def tile_step(j, state, mask):
        start = pl.multiple_of(j * tq, tq)
        kt = k_ref[pl.ds(start, tq), :]
        vt = v_ref[pl.ds(start, tq), :]
        new = []
        for h in range(HEADS_PER_GROUP):
            carry, acc = state[h]
            z = _dot_nt(qh[h], kt) + bh[h]
            pv, carry = _stick_tile(z, carry, u, vt, mask)
            new.append((carry, acc + pv))
        return tuple(new)

    init = tuple((jnp.zeros((tq, LANES), F32), jnp.zeros((tq, LANES), F32)) for _ in range(HEADS_PER_GROUP))
    state = tile_step(i, init, causal)
    state = lax.fori_loop(0, i, lambda s, st: tile_step(i - 1 - s, st, None), state)
    out = state[0][1]
    for h in range(1, HEADS_PER_GROUP):
        out = jnp.where((lane // HEAD_DIM) == h, state[h][1], out)
    o_ref[...] = out.astype(o_ref.dtype)


def _attn_prompt(bias, q, k, v, tq):
    b, t, d = q.shape
    u = _suffix_sum_matrix(tq)
    return pl.pallas_call(
        functools.partial(_attn_prompt_body, tq=tq),
        grid=(b, d // LANES, t // tq),
        in_specs=[pl.BlockSpec(memory_space=pltpu.SMEM),
                  pl.BlockSpec((None, tq, LANES), lambda bi, hp, i: (bi, i, hp)),
                  pl.BlockSpec((None, t, LANES), lambda bi, hp, i: (bi, 0, hp)),
                  pl.BlockSpec((None, t, LANES), lambda bi, hp, i: (bi, 0, hp)),
                  _resident(u.shape)],
        out_specs=pl.BlockSpec((None, tq, LANES), lambda bi, hp, i: (bi, i, hp)),
        out_shape=jax.ShapeDtypeStruct((b, t, d), BF16),
        compiler_params=_params(3),
        name="attn_prompt",
    )(bias, q, k, v, u)


def _attn_sample_body(pt_ref, q_ref, kn_ref, vn_ref, bias_ref, u_ref, *rest, t_seq, pages_per_step):
    kp_refs = rest[:pages_per_step]
    vp_refs = rest[pages_per_step:2 * pages_per_step]
    o_ref = rest[2 * pages_per_step]
    qbd_scr, carry_scr, acc_scr = rest[2 * pages_per_step + 1:]
    g = pl.program_id(1)
    rows = N_HEADS * t_seq
    d = acc_scr.shape[1]
    page = kp_refs[0].shape[0]
    own_head = (lax.broadcasted_iota(jnp.int32, (rows, d), 0) // t_seq
                == lax.broadcasted_iota(jnp.int32, (rows, d), 1) // HEAD_DIM)
    u = u_ref[...]

    def page_step(kp, vp, mask):
        z = _dot_nt(qbd_scr[...], kp) + bias_ref[...]
        pv, carry = _stick_tile(z, carry_scr[...], u, vp, mask)
        acc_scr[...] += pv
        carry_scr[...] = carry

    @pl.when(g == 0)
    def _():
        qt = jnp.concatenate([q_ref[...]] * N_HEADS, axis=0)
        qbd_scr[...] = jnp.where(own_head, qt, 0.0).astype(BF16)
        carry_scr[...] = jnp.zeros_like(carry_scr)
        acc_scr[...] = jnp.zeros_like(acc_scr)
        pad = jnp.zeros((page - t_seq, d), F32)
        kc = jnp.concatenate([kn_ref[...], pad], axis=0).astype(BF16)
        vc = jnp.concatenate([vn_ref[...], pad], axis=0).astype(BF16)
        key_idx = lax.broadcasted_iota(jnp.int32, (rows, page), 1)
        qry_idx = lax.broadcasted_iota(jnp.int32, (rows, page), 0) % t_seq
        page_step(kc, vc, key_idx < qry_idx)

    for p in range(pages_per_step):
        page_step(kp_refs[p][...].astype(BF16), vp_refs[p][...].astype(BF16), None)

    @pl.when(g == pl.num_programs(1) - 1)
    def _():
        m = jnp.where(own_head, acc_scr[...], 0.0)
        out = m[0:t_seq, :]
        for h in range(1, N_HEADS):
            out = out + m[h * t_seq:(h + 1) * t_seq, :]
        o_ref[...] = out


def _attn_sample(page_table, q, k_new, v_new, bias_rows, cache_k, cache_v, layer, t_seq, pages_per_step):
    n, d = q.shape
    n_seq, n_pages = page_table.shape
    page = cache_k.shape[2]
    rows = N_HEADS * t_seq
    u = _suffix_sum_matrix(page)
    seq_rows = pl.BlockSpec((t_seq, d), lambda b, g, pt: (b, 0))

    def page_spec(p):
        def index(b, g, pt):
            return (layer, pt[b * n_pages + (n_pages - 1 - (g * pages_per_step + p))], 0, 0)
        return pl.BlockSpec((None, None, page, d), index)

    const = lambda a: pl.BlockSpec(a.shape, lambda b, g, pt: (0,) * a.ndim)
    grid_spec = pltpu.PrefetchScalarGridSpec(
        num_scalar_prefetch=1,
        grid=(n_seq, n_pages // pages_per_step),
        in_specs=[seq_rows, seq_rows, seq_rows, const(bias_rows), const(u)]
                 + [page_spec(p) for p in range(pages_per_step)] * 2,
        out_specs=seq_rows,
        scratch_shapes=[pltpu.VMEM((rows, d), BF16), pltpu.VMEM((rows, LANES), F32),
                        pltpu.VMEM((rows, d), F32)],
    )
    return pl.pallas_call(
        functools.partial(_attn_sample_body, t_seq=t_seq, pages_per_step=pages_per_step),
        grid_spec=grid_spec,
        out_shape=jax.ShapeDtypeStruct((n, d), F32),
        compiler_params=_params(2),
        name="attn_sample",
    )(page_table.reshape(-1), q, k_new, v_new, bias_rows, u,
      *([cache_k] * pages_per_step), *([cache_v] * pages_per_step))


def _merge_body(x_ref, sa_ref, o_ref, cp_ref, pre_ref, post_ref, wgate_ref, wa_ref, wb_ref, wc_ref,
                wo_ref, y_ref):
    x = x_ref[...]
    d = x.shape[-1]
    u = _rms(x, pre_ref[...]).astype(BF16)
    merged = jax.nn.sigmoid(_dot(u, wgate_ref[:, 0:d])) * _dot(sa_ref[...], wa_ref[...])
    merged = merged + jax.nn.sigmoid(_dot(u, wgate_ref[:, d:2 * d])) * _dot(o_ref[...].astype(BF16), wb_ref[...])
    merged = merged + jax.nn.sigmoid(_dot(u, wgate_ref[:, 2 * d:3 * d])) * _dot(cp_ref[...], wc_ref[...])
    m = _dot(merged.astype(BF16), wo_ref[...])
    y_ref[...] = x + _rms(m, post_ref[...])


def _merge(x, sa, o, cp, pre_g, post_g, w_gate, w_a, w_b, w_c, w_o, tile):
    n, d = x.shape
    rows = lambda w: pl.BlockSpec((tile, w), lambda i: (i, 0))
    full = lambda a: _resident(a.shape)
    return pl.pallas_call(
        _merge_body,
        grid=(n // tile,),
        in_specs=[rows(d), rows(sa.shape[1]), rows(o.shape[1]), rows(cp.shape[1]),
                  full(pre_g), full(post_g), full(w_gate), full(w_a), full(w_b), full(w_c), full(w_o)],
        out_specs=rows(d),
        out_shape=jax.ShapeDtypeStruct((n, d), F32),
        compiler_params=_params(1),
        name="merge",
    )(x, sa, o, cp, pre_g, post_g, w_gate, w_a, w_b, w_c, w_o)


ROW_TILE = 512
MIX_TILE = 256
ATTN_TILE = 256
PAGES_PER_STEP = 8


def _pad_rows_front(a, multiple):
    r = a.shape[-2]
    pad = -r % multiple
    return jnp.pad(a, [(0, 0)] * (a.ndim - 2) + [(pad, 0), (0, 0)])


def kernel(x_prompt, x_sample, cache_k, cache_v, state_conv_a, state_conv_c, page_table, ffn1_pre_g, ffn1_post_g, ffn1_w_gate, ffn1_w_up, ffn1_w_down, mix_pre_g, mix_post_g, w_in, conv_a_w, conv_a_b, ln_a_g, ln_a_b, w_a_out, w_b_out, sb_bias, conv_c_w, w_c_out, w_o, ffn2_pre_g, ffn2_post_g, ffn2_w_gate, ffn2_w_up, ffn2_w_down):
    b, t, d = x_prompt.shape
    n_seq, t_seq, _ = x_sample.shape
    depth, n_pool, page = cache_k.shape[:3]
    wa, d_a = conv_a_w.shape[1:]
    wc, d_c = conv_c_w.shape[1:]
    n_main = _mix_columns(d_a, d_c)[-1]

    xp = x_prompt.reshape(b * t, d)
    xs = x_sample.reshape(n_seq * t_seq, d)
    ck = cache_k.reshape(depth, n_pool, page, D_ATTN)
    cv = cache_v.reshape(depth, n_pool, page, D_ATTN)
    zero_a = jnp.zeros((b, -(-(wa - 1) // SUBLANES) * SUBLANES, d_a), F32)
    zero_c = jnp.zeros((b, -(-(wc - 1) // SUBLANES) * SUBLANES, d_c), F32)
    sta_pad = _pad_rows_front(state_conv_a, SUBLANES)
    stc_pad = _pad_rows_front(state_conv_c, SUBLANES)
    vec = lambda a: a.reshape(1, -1)

    outs = [[] for _ in range(8)]
    for l in range(depth):
        f1 = (vec(ffn1_pre_g[l]), vec(ffn1_post_g[l]), ffn1_w_gate[l].astype(BF16),
              ffn1_w_up[l].astype(BF16), ffn1_w_down[l].astype(BF16))
        f2 = (vec(ffn2_pre_g[l]), vec(ffn2_post_g[l]), ffn2_w_gate[l].astype(BF16),
              ffn2_w_up[l].astype(BF16), ffn2_w_down[l].astype(BF16))
        w_main = w_in[l, :, :n_main].astype(BF16)
        w_gate = w_in[l, :, n_main:].astype(BF16)
        conv = (conv_a_w[l], vec(conv_a_b[l]), vec(ln_a_g[l]), vec(ln_a_b[l]), conv_c_w[l])
        merge_w = (vec(mix_pre_g[l]), vec(mix_post_g[l]), w_gate, w_a_out[l].astype(BF16),
                   w_b_out[l].astype(BF16), w_c_out[l].astype(BF16), w_o[l].astype(BF16))
        bias_rows = jnp.broadcast_to(jnp.repeat(sb_bias[l], t_seq)[:, None], (N_HEADS * t_seq, LANES))

        xp = _ffn(xp, *f1, tile=ROW_TILE)
        q, k, v, kb, vb, sa, cp, na, nc = _mixproj_prompt(
            xp.reshape(b, t, d), vec(mix_pre_g[l]), w_main, zero_a, zero_c, *conv, tile=MIX_TILE)
        o = _attn_prompt(sb_bias[l], q, kb, vb, tq=ATTN_TILE)
        xp = _merge(xp, sa.reshape(b * t, d_a), o.reshape(b * t, D_ATTN), cp.reshape(b * t, d_c),
                    *merge_w, tile=ROW_TILE)
        xp = _ffn(xp, *f2, tile=ROW_TILE)
        for lst, val in zip(outs[:4], (k.reshape(b, t, N_HEADS, HEAD_DIM), v.reshape(b, t, N_HEADS, HEAD_DIM),
                                       na, nc)):
            lst.append(val)

        xs = _ffn(xs, *f1, tile=n_seq * t_seq)
        q, k, v, sa, cp, na, nc = _mixproj_sample(
            xs, vec(mix_pre_g[l]), w_main, sta_pad[l], stc_pad[l], *conv, n_seq=n_seq, t_seq=t_seq)
        o = _attn_sample(page_table, q, k, v, bias_rows, ck, cv, l, t_seq, PAGES_PER_STEP)
        xs = _merge(xs, sa, o, cp, *merge_w, tile=n_seq * t_seq)
        xs = _ffn(xs, *f2, tile=n_seq * t_seq)
        for lst, val in zip(outs[4:], (k.reshape(n_seq, t_seq, N_HEADS, HEAD_DIM),
                                       v.reshape(n_seq, t_seq, N_HEADS, HEAD_DIM), na, nc)):
            lst.append(val)

    return (xp.reshape(b, t, d), xs.reshape(n_seq, t_seq, d), *[jnp.stack(o) for o in outs])
```

```python
import functools

import jax
import jax.numpy as jnp
from jax import lax
from jax.experimental import pallas as pl
from jax.experimental.pallas import tpu as pltpu

EPS = 1e-6
N_HEADS = 16
HEAD_DIM = 64
D_ATTN = N_HEADS * HEAD_DIM
LANES = 128
SUBLANES = 8
VMEM_LIMIT = 56 * 1024 * 1024
SOFTPLUS_LINEAR_ABOVE = 60.0

F32 = jnp.float32
BF16 = jnp.bfloat16


def _params(n_grid):
    return pltpu.CompilerParams(dimension_semantics=("arbitrary",) * n_grid,
                                vmem_limit_bytes=VMEM_LIMIT)


def _resident(shape):
    zeros = (0,) * len(shape)
    return pl.BlockSpec(shape, lambda *_: zeros, pipeline_mode=pl.Buffered(1))


def _rms(x, g):
    return x * lax.rsqrt(jnp.mean(x * x, axis=-1, keepdims=True) + EPS) * g


def _silu(x):
    return x * jax.nn.sigmoid(x)


def _softplus(z):
    return jnp.where(z > SOFTPLUS_LINEAR_ABOVE, z, jnp.log(1.0 + jnp.exp(z)))


def _dot(a, b):
    return jnp.dot(a, b, preferred_element_type=F32)


def _dot_nt(a, b):
    return lax.dot_general(a, b, (((1,), (1,)), ((), ())), preferred_element_type=F32)


def _dot_tn(a, b):
    return lax.dot_general(a, b, (((0,), (0,)), ((), ())), preferred_element_type=F32)


def _ffn_body(x_ref, pre_ref, post_ref, wg_ref, wu_ref, wd_ref, o_ref):
    x = x_ref[...]
    xn = _rms(x, pre_ref[...]).astype(BF16)
    h = (_silu(_dot(xn, wg_ref[...])) * _dot(xn, wu_ref[...])).astype(BF16)
    y = _dot(h, wd_ref[...])
    o_ref[...] = x + 0.5 * _rms(y, post_ref[...])


def _ffn(x, pre_g, post_g, wg, wu, wd, tile):
    n, d = x.shape
    f = wg.shape[1]
    row = pl.BlockSpec((tile, d), lambda i: (i, 0))
    return pl.pallas_call(
        _ffn_body,
        grid=(n // tile,),
        in_specs=[row, _resident((1, d)), _resident((1, d)),
                  _resident((d, f)), _resident((d, f)), _resident((f, d))],
        out_specs=row,
        out_shape=jax.ShapeDtypeStruct((n, d), F32),
        compiler_params=_params(1),
        name="ffn",
    )(x, pre_g, post_g, wg, wu, wd)


def _layernorm_silu(t, g, b):
    mu = jnp.mean(t, axis=-1, keepdims=True)
    c = t - mu
    var = jnp.mean(c * c, axis=-1, keepdims=True)
    return _silu(c * lax.rsqrt(var + EPS) * g + b)


def _conv_taps(w_ref, b, scr, phase_scr, first, rows, n_taps, lead=()):
    acc = b
    for phase in range(SUBLANES):
        taps = [j for j in range(n_taps) if (first + j) % SUBLANES == phase]
        if not taps:
            continue
        lo = first + taps[0]
        span = taps[-1] - taps[0] + rows
        if phase == 0:
            src, base = scr, lo
        else:
            phase_scr[lead + (slice(0, span), slice(None))] = scr[lead + (slice(lo, lo + span), slice(None))]
            src, base = phase_scr, 0
        for j in taps:
            off = base + j - taps[0]
            term = w_ref[j:j + 1, :] * src[lead + (slice(off, off + rows), slice(None))]
            acc = term if acc is None else acc + term
    return acc


def _mixproj_prompt_body(x_ref, pre_ref, wa_ref, wqt_ref, wk_ref, wvt_ref, wc_ref, sta_ref, stc_ref,
                         caw_ref, cab_ref, lng_ref, lnb_ref, ccw_ref,
                         qt_ref, kb_ref, kt_ref, vt_ref, vtb_ref, sa_ref, cp_ref, na_ref, nc_ref,
                         xa_scr, xc_scr, ph_scr, *, tile, wa, wc, pad_a, pad_c):
    i = pl.program_id(1)
    halo_a = wa - 1
    halo_c = wc - 1
    d_a = xa_scr.shape[1]
    d_c = xc_scr.shape[1]

    @pl.when(i == 0)
    def _():
        xa_scr[0:pad_a, :] = sta_ref[...]
        xc_scr[0:pad_c, :] = stc_ref[...]

    @pl.when(i > 0)
    def _():
        xa_scr[pad_a - halo_a:pad_a, :] = xa_scr[tile + pad_a - halo_a:tile + pad_a, :]
        xc_scr[pad_c - halo_c:pad_c, :] = xc_scr[tile + pad_c - halo_c:tile + pad_c, :]

    u = _rms(x_ref[...], pre_ref[...]).astype(BF16)
    qt_ref[...] = (_dot_nt(wqt_ref[...], u) * (HEAD_DIM ** -0.5)).astype(BF16)
    k = _dot(u, wk_ref[...])
    kb_ref[...] = k.astype(BF16)
    kt_ref[...] = k.T
    vt = _dot_nt(wvt_ref[...], u)
    vt_ref[...] = vt
    vtb_ref[...] = vt.astype(BF16)

    a2 = _dot(u, wa_ref[...])
    xa_scr[pad_a:pad_a + tile, :] = a2[:, :d_a] * jax.nn.sigmoid(a2[:, d_a:])
    conv = _conv_taps(caw_ref, cab_ref[...], xa_scr, ph_scr, pad_a - halo_a, tile, wa)
    sa_ref[...] = _layernorm_silu(conv, lng_ref[...], lnb_ref[...]).astype(BF16)

    c3 = _dot(u, wc_ref[...])
    xc_scr[pad_c:pad_c + tile, :] = c3[:, 2 * d_c:] * c3[:, :d_c]
    cconv = _conv_taps(ccw_ref, None, xc_scr, ph_scr, pad_c - halo_c, tile, wc)
    cp_ref[...] = (c3[:, d_c:2 * d_c] * cconv).astype(BF16)

    @pl.when(i == pl.num_programs(1) - 1)
    def _():
        na_ref[...] = xa_scr[tile + pad_a - halo_a:tile + pad_a, :]
        nc_ref[...] = xc_scr[tile + pad_c - halo_c:tile + pad_c, :]


def _mixproj_prompt(x, pre_g, w_a, w_qt, w_k, w_vt, w_c, sta_pad, stc_pad, caw, cab, lng, lnb, ccw, tile):
    b, t, d = x.shape
    wa, d_a = caw.shape
    wc, d_c = ccw.shape
    pad_a, pad_c = sta_pad.shape[1], stc_pad.shape[1]
    rows = lambda w: pl.BlockSpec((None, tile, w), lambda bi, i: (bi, i, 0))
    cols = pl.BlockSpec((None, D_ATTN, tile), lambda bi, i: (bi, 0, i))
    per_seq = lambda r, w: pl.BlockSpec((None, r, w), lambda bi, i: (bi, 0, 0))
    full = lambda a: _resident(a.shape)
    body = functools.partial(_mixproj_prompt_body, tile=tile, wa=wa, wc=wc, pad_a=pad_a, pad_c=pad_c)
    return pl.pallas_call(
        body,
        grid=(b, t // tile),
        in_specs=[rows(d), full(pre_g), full(w_a), full(w_qt), full(w_k), full(w_vt), full(w_c),
                  per_seq(pad_a, d_a), per_seq(pad_c, d_c),
                  full(caw), full(cab), full(lng), full(lnb), full(ccw)],
        out_specs=[cols, rows(D_ATTN), cols, cols,
                   pl.BlockSpec((None, None, D_ATTN, tile), lambda bi, i: (bi, i, 0, 0)),
                   rows(d_a), rows(d_c), per_seq(wa - 1, d_a), per_seq(wc - 1, d_c)],
        out_shape=[jax.ShapeDtypeStruct((b, D_ATTN, t), BF16),
                   jax.ShapeDtypeStruct((b, t, D_ATTN), BF16),
                   jax.ShapeDtypeStruct((b, D_ATTN, t), F32),
                   jax.ShapeDtypeStruct((b, D_ATTN, t), F32),
                   jax.ShapeDtypeStruct((b, t // tile, D_ATTN, tile), BF16),
                   jax.ShapeDtypeStruct((b, t, d_a), BF16),
                   jax.ShapeDtypeStruct((b, t, d_c), BF16),
                   jax.ShapeDtypeStruct((b, wa - 1, d_a), F32),
                   jax.ShapeDtypeStruct((b, wc - 1, d_c), F32)],
        scratch_shapes=[pltpu.VMEM((pad_a + tile, d_a), F32), pltpu.VMEM((pad_c + tile, d_c), F32),
                        pltpu.VMEM((pad_a + tile, max(d_a, d_c)), F32)],
        compiler_params=_params(2),
        name="mixproj_prompt",
    )(x, pre_g, w_a, w_qt, w_k, w_vt, w_c, sta_pad, stc_pad, caw, cab, lng, lnb, ccw)


def _mixproj_sample_body(x_ref, pre_ref, wa_ref, wqt_ref, wk_ref, wvt_ref, wc_ref, sta_ref, stc_ref,
                         caw_ref, cab_ref, lng_ref, lnb_ref, ccw_ref,
                         q_ref, k_ref, v_ref, sa_ref, cp_ref, na_ref, nc_ref,
                         xa_scr, xc_scr, ph_scr, *, n_seq, t_seq, wa, wc, pad_a, pad_c):
    halo_a = wa - 1
    halo_c = wc - 1
    d_a = xa_scr.shape[2]
    d_c = xc_scr.shape[2]
    n = n_seq * t_seq
    u = _rms(x_ref[...], pre_ref[...]).astype(BF16)
    q_ref[...] = _dot_nt(u, wqt_ref[...]) * (HEAD_DIM ** -0.5)
    k_ref[...] = _dot(u, wk_ref[...])
    v_ref[...] = _dot_nt(u, wvt_ref[...])

    a2 = _dot(u, wa_ref[...])
    xa_scr[:, 0:pad_a, :] = sta_ref[...]
    xa_scr[:, pad_a:pad_a + t_seq, :] = (a2[:, :d_a] * jax.nn.sigmoid(a2[:, d_a:])).reshape(n_seq, t_seq, d_a)
    conv = _conv_taps(caw_ref, cab_ref[...], xa_scr, ph_scr, pad_a - halo_a, t_seq, wa, lead=(slice(None),))
    sa_ref[...] = _layernorm_silu(conv, lng_ref[...], lnb_ref[...]).reshape(n, d_a).astype(BF16)

    c3 = _dot(u, wc_ref[...])
    xc_scr[:, 0:pad_c, :] = stc_ref[...]
    xc_scr[:, pad_c:pad_c + t_seq, :] = (c3[:, 2 * d_c:] * c3[:, :d_c]).reshape(n_seq, t_seq, d_c)
    cconv = _conv_taps(ccw_ref, None, xc_scr, ph_scr, pad_c - halo_c, t_seq, wc, lead=(slice(None),))
    cp_ref[...] = (c3[:, d_c:2 * d_c] * cconv.reshape(n, d_c)).astype(BF16)

    na_ref[...] = xa_scr[:, pad_a + t_seq - halo_a:pad_a + t_seq, :]
    nc_ref[...] = xc_scr[:, pad_c + t_seq - halo_c:pad_c + t_seq, :]


def _mixproj_sample(x, pre_g, w_a, w_qt, w_k, w_vt, w_c, sta_pad, stc_pad, caw, cab, lng, lnb, ccw,
                    n_seq, t_seq):
    n, d = x.shape
    wa, d_a = caw.shape
    wc, d_c = ccw.shape
    pad_a, pad_c = sta_pad.shape[1], stc_pad.shape[1]
    body = functools.partial(_mixproj_sample_body, n_seq=n_seq, t_seq=t_seq, wa=wa, wc=wc,
                             pad_a=pad_a, pad_c=pad_c)
    full = lambda a: _resident(a.shape)
    out_shape = [jax.ShapeDtypeStruct((n, D_ATTN), F32),
                 jax.ShapeDtypeStruct((n, D_ATTN), F32),
                 jax.ShapeDtypeStruct((n, D_ATTN), F32),
                 jax.ShapeDtypeStruct((n, d_a), BF16),
                 jax.ShapeDtypeStruct((n, d_c), BF16),
                 jax.ShapeDtypeStruct((n_seq, wa - 1, d_a), F32),
                 jax.ShapeDtypeStruct((n_seq, wc - 1, d_c), F32)]
    args = (x, pre_g, w_a, w_qt, w_k, w_vt, w_c, sta_pad, stc_pad, caw, cab, lng, lnb, ccw)
    return pl.pallas_call(
        body,
        grid=(1,),
        in_specs=[full(a) for a in args],
        out_specs=[pl.BlockSpec(s.shape, lambda i, nd=len(s.shape): (0,) * nd) for s in out_shape],
        out_shape=out_shape,
        scratch_shapes=[pltpu.VMEM((n_seq, pad_a + t_seq, d_a), F32),
                        pltpu.VMEM((n_seq, pad_c + t_seq, d_c), F32),
                        pltpu.VMEM((n_seq, pad_a + t_seq, max(d_a, d_c)), F32)],
        compiler_params=_params(1),
        name="mixproj_sample",
    )(*args)


def _attn_prompt_body(bias_ref, qt_ref, k_ref, vt_ref, tri_ref, ot_ref, z_scr, a_scr, *, tile, heads):
    g = pl.program_id(1)
    i = pl.program_id(2)
    tri = tri_ref[...]
    qt = qt_ref[...]
    zero = jnp.zeros((HEAD_DIM, tile), BF16)
    pair = LANES // HEAD_DIM
    qm = []
    for h in range(heads):
        parts = [zero] * pair
        parts[h % pair] = qt[h * HEAD_DIM:(h + 1) * HEAD_DIM, :]
        qm.append(jnp.concatenate(parts, axis=0))
    ones = jnp.ones((tile, LANES), BF16)
    row = lax.broadcasted_iota(jnp.int32, (LANES, tile), 0)
    qe = []
    for h in range(heads):
        b = jnp.full((LANES, tile), bias_ref[g * heads + h], F32)
        hi = b.astype(BF16).astype(F32)
        mid = (b - hi).astype(BF16).astype(F32)
        lo = b - hi - mid
        piece = jnp.where(row == 0, hi, jnp.where(row == 1, mid, jnp.where(row == 2, lo, 0.0)))
        qe.append(jnp.concatenate([qm[h], piece.astype(BF16)], axis=0))
    key_before_query = (lax.broadcasted_iota(jnp.int32, (tile, tile), 0)
                        < lax.broadcasted_iota(jnp.int32, (tile, tile), 1))

    def logits(j):
        kt = k_ref[pl.ds(pl.multiple_of(j * tile, tile), tile), :]
        return [_dot(jnp.concatenate([kt[:, (h // pair) * LANES:(h // pair + 1) * LANES], ones], axis=1), qe[h])
                for h in range(heads)]

    def weights(zs, carries, mask):
        sps = []
        for h in range(heads):
            sp = _softplus(zs[h])
            if mask is not None:
                sp = jnp.where(mask, sp, 0.0)
            sps.append(sp.astype(BF16))
        incls = [_dot(tri, sps[h]) for h in range(heads)]
        ws = []
        for h in range(heads):
            a = jnp.exp(zs[h] - incls[h] - carries[h])
            if mask is not None:
                a = jnp.where(mask, a, 0.0)
            ws.append(a.astype(BF16))
        return ws, [carries[h] + incls[h][0:1, :] for h in range(heads)]

    def values(j, ws, accs):
        vt = vt_ref[j]
        return [accs[h] + _dot(vt[h * HEAD_DIM:(h + 1) * HEAD_DIM, :], ws[h]) for h in range(heads)]

    carries = [jnp.zeros((1, tile), F32)] * heads
    accs = [jnp.zeros((HEAD_DIM, tile), F32)] * heads

    def load(scr, slot):
        return [scr[slot, h] for h in range(heads)]

    def store(scr, slot, xs):
        for h in range(heads):
            scr[slot, h] = xs[h]

    def stage(j, slot, carries, accs, look_ahead=True):
        accs = values(j + 1, load(a_scr, 1 - slot), accs)
        if look_ahead:
            store(z_scr, 1 - slot, logits(jnp.maximum(j - 1, 0)))
        ws, carries = weights(load(z_scr, slot), carries, None)
        store(a_scr, slot, ws)
        return carries, accs

    z_diag = logits(i)
    store(z_scr, 0, logits(jnp.maximum(i - 1, 0)))
    ws, carries = weights(z_diag, carries, key_before_query)
    store(a_scr, 1, ws)

    def trip(s, state):
        j = i - 1 - 2 * s
        carries, accs = stage(j, 0, *state)
        return stage(j - 1, 1, carries, accs)

    carries, accs = lax.fori_loop(0, i // 2, trip, (carries, accs))

    def odd_tail(carries, accs):
        _, accs = stage(0, 0, carries, accs, look_ahead=False)
        return values(0, load(a_scr, 0), accs)

    accs = lax.cond(i % 2 == 1, odd_tail, lambda carries, accs: values(0, load(a_scr, 1), accs), carries, accs)
    ot_ref[...] = jnp.concatenate(accs, axis=0).astype(BF16)


def _attn_prompt(bias, qt, k, vtb, tile, heads):
    b, d, t = qt.shape
    rows = heads * HEAD_DIM
    n_tiles = t // tile
    tri = (jnp.arange(tile)[None, :] >= jnp.arange(tile)[:, None]).astype(BF16)
    return pl.pallas_call(
        functools.partial(_attn_prompt_body, tile=tile, heads=heads),
        grid=(b, d // rows, n_tiles),
        in_specs=[pl.BlockSpec(memory_space=pltpu.SMEM),
                  pl.BlockSpec((None, rows, tile), lambda bi, g, i: (bi, g, i)),
                  pl.BlockSpec((None, t, rows), lambda bi, g, i: (bi, 0, g)),
                  pl.BlockSpec((None, n_tiles, rows, tile), lambda bi, g, i: (bi, 0, g, 0)),
                  _resident(tri.shape)],
        out_specs=pl.BlockSpec((None, rows, tile), lambda bi, g, i: (bi, g, i)),
        out_shape=jax.ShapeDtypeStruct((b, d, t), BF16),
        scratch_shapes=[pltpu.VMEM((2, heads, tile, tile), F32), pltpu.VMEM((2, heads, tile, tile), BF16)],
        compiler_params=_params(3),
        name="attn_prompt",
    )(bias, qt, k, vtb, tri)


def _attn_sample_body(pt_ref, q_ref, kn_ref, vn_ref, bias_ref, tri_ref, *rest, t_seq, pages_per_step):
    kp_refs = rest[:pages_per_step]
    vp_refs = rest[pages_per_step:2 * pages_per_step]
    o_ref = rest[2 * pages_per_step]
    qbd_scr, carry_scr, acc_scr = rest[2 * pages_per_step + 1:]
    g = pl.program_id(1)
    rows = N_HEADS * t_seq
    d = acc_scr.shape[1]
    page = kp_refs[0].shape[1]
    own_head = (lax.broadcasted_iota(jnp.int32, (rows, d), 0) // t_seq
                == lax.broadcasted_iota(jnp.int32, (rows, d), 1) // HEAD_DIM)
    tri = tri_ref[...]

    def weights(z, carry, mask):
        sp = _softplus(z)
        if mask is not None:
            sp = jnp.where(mask, sp, 0.0)
        sp = sp.astype(BF16)
        ws = []
        for p in range(z.shape[1] // page):
            cols = slice(p * page, (p + 1) * page)
            r = _dot(sp[:, cols], tri)
            a = jnp.exp(z[:, cols] - r[:, :page] - carry)
            if mask is not None:
                a = jnp.where(mask, a, 0.0)
            ws.append(a.astype(BF16))
            carry = carry + r[:, page:]
        return (ws[0] if len(ws) == 1 else jnp.concatenate(ws, axis=1)), carry

    @pl.when(g == 0)
    def _():
        qt = jnp.concatenate([q_ref[...]] * N_HEADS, axis=0)
        qbd = jnp.where(own_head, qt, 0.0).astype(BF16)
        qbd_scr[...] = qbd
        pad = jnp.zeros((page - t_seq, d), F32)
        kc = jnp.concatenate([kn_ref[...], pad], axis=0).astype(BF16)
        vc = jnp.concatenate([vn_ref[...], pad], axis=0).astype(BF16)
        key_idx = lax.broadcasted_iota(jnp.int32, (rows, page), 1)
        qry_idx = lax.broadcasted_iota(jnp.int32, (rows, page), 0) % t_seq
        w, carry = weights(_dot_nt(qbd, kc) + bias_ref[...], jnp.zeros((rows, LANES), F32), key_idx < qry_idx)
        acc_scr[...] = _dot(w, vc)
        carry_scr[...] = carry

    k_all = jnp.concatenate([r[...].astype(BF16) for r in kp_refs], axis=1)
    v_all = jnp.concatenate([r[...].astype(BF16) for r in vp_refs], axis=1)
    bias_all = jnp.concatenate([bias_ref[...]] * pages_per_step, axis=1)
    w, carry = weights(_dot(qbd_scr[...], k_all) + bias_all, carry_scr[...], None)
    acc_scr[...] += _dot_nt(w, v_all)
    carry_scr[...] = carry

    @pl.when(g == pl.num_programs(1) - 1)
    def _():
        m = jnp.where(own_head, acc_scr[...], 0.0)
        out = m[0:t_seq, :]
        for h in range(1, N_HEADS):
            out = out + m[h * t_seq:(h + 1) * t_seq, :]
        o_ref[...] = out


def _attn_sample(page_table, q, k_new, v_new, bias_rows, cache_kt, cache_vt, layer, t_seq, pages_per_step):
    n, d = q.shape
    n_seq, n_pages = page_table.shape
    page = cache_kt.shape[3]
    rows = N_HEADS * t_seq
    tri = jnp.concatenate([(jnp.arange(page)[:, None] >= jnp.arange(page)[None, :]).astype(BF16),
                           jnp.ones((page, LANES), BF16)], axis=1)
    seq_rows = pl.BlockSpec((t_seq, d), lambda b, g, pt: (b, 0))

    def page_spec(p):
        def index(b, g, pt):
            return (layer, pt[b * n_pages + (n_pages - 1 - (g * pages_per_step + p))], 0, 0)
        return pl.BlockSpec((None, None, d, page), index)

    const = lambda a: pl.BlockSpec(a.shape, lambda b, g, pt: (0,) * a.ndim)
    grid_spec = pltpu.PrefetchScalarGridSpec(
        num_scalar_prefetch=1,
        grid=(n_seq, n_pages // pages_per_step),
        in_specs=[seq_rows, seq_rows, seq_rows, const(bias_rows), const(tri)]
                 + [page_spec(p) for p in range(pages_per_step)] * 2,
        out_specs=seq_rows,
        scratch_shapes=[pltpu.VMEM((rows, d), BF16), pltpu.VMEM((rows, LANES), F32),
                        pltpu.VMEM((rows, d), F32)],
    )
    return pl.pallas_call(
        functools.partial(_attn_sample_body, t_seq=t_seq, pages_per_step=pages_per_step),
        grid_spec=grid_spec,
        out_shape=jax.ShapeDtypeStruct((n, d), F32),
        compiler_params=_params(2),
        name="attn_sample",
    )(page_table.reshape(-1), q, k_new, v_new, bias_rows, tri,
      *([cache_kt] * pages_per_step), *([cache_vt] * pages_per_step))


def _merge_body(x_ref, sa_ref, o_ref, cp_ref, pre_ref, post_ref, wgate_ref, wa_ref, wb_ref, wc_ref,
                wo_ref, y_ref, *, o_feature_major):
    x = x_ref[...]
    d = x.shape[-1]
    u = _rms(x, pre_ref[...]).astype(BF16)
    if o_feature_major:
        branch_b = _dot_tn(o_ref[...], wb_ref[...])
    else:
        branch_b = _dot(o_ref[...].astype(BF16), wb_ref[...])
    merged = jax.nn.sigmoid(_dot(u, wgate_ref[:, 0:d])) * _dot(sa_ref[...], wa_ref[...])
    merged = merged + jax.nn.sigmoid(_dot(u, wgate_ref[:, d:2 * d])) * branch_b
    merged = merged + jax.nn.sigmoid(_dot(u, wgate_ref[:, 2 * d:3 * d])) * _dot(cp_ref[...], wc_ref[...])
    m = _dot(merged.astype(BF16), wo_ref[...])
    y_ref[...] = x + _rms(m, post_ref[...])


def _merge(x, sa, o, cp, pre_g, post_g, w_gate, w_a, w_b, w_c, w_o, tile, o_feature_major):
    b, t, d = x.shape
    rows = lambda w: pl.BlockSpec((None, tile, w), lambda bi, i: (bi, i, 0))
    o_spec = (pl.BlockSpec((None, o.shape[1], tile), lambda bi, i: (bi, 0, i)) if o_feature_major
              else rows(o.shape[2]))
    full = lambda a: _resident(a.shape)
    return pl.pallas_call(
        functools.partial(_merge_body, o_feature_major=o_feature_major),
        grid=(b, t // tile),
        in_specs=[rows(d), rows(sa.shape[2]), o_spec, rows(cp.shape[2]),
                  full(pre_g), full(post_g), full(w_gate), full(w_a), full(w_b), full(w_c), full(w_o)],
        out_specs=rows(d),
        out_shape=jax.ShapeDtypeStruct((b, t, d), F32),
        compiler_params=_params(2),
        name="merge",
    )(x, sa, o, cp, pre_g, post_g, w_gate, w_a, w_b, w_c, w_o)


ROW_TILE = 512
MIX_TILE = 256
ATTN_TILE = 256
ATTN_HEADS = 4
PAGES_PER_STEP = 8


def _pad_rows_front(a, multiple):
    r = a.shape[-2]
    pad = -r % multiple
    return jnp.pad(a, [(0, 0)] * (a.ndim - 2) + [(pad, 0), (0, 0)])


def kernel(x_prompt, x_sample, cache_k, cache_v, state_conv_a, state_conv_c, page_table, ffn1_pre_g, ffn1_post_g, ffn1_w_gate, ffn1_w_up, ffn1_w_down, mix_pre_g, mix_post_g, w_in, conv_a_w, conv_a_b, ln_a_g, ln_a_b, w_a_out, w_b_out, sb_bias, conv_c_w, w_c_out, w_o, ffn2_pre_g, ffn2_post_g, ffn2_w_gate, ffn2_w_up, ffn2_w_down):
    b, t, d = x_prompt.shape
    n_seq, t_seq, _ = x_sample.shape
    depth, n_pool, page = cache_k.shape[:3]
    wa, d_a = conv_a_w.shape[1:]
    wc, d_c = conv_c_w.shape[1:]
    n = n_seq * t_seq
    o_q = 2 * d_a
    o_k = o_q + D_ATTN
    o_v = o_k + D_ATTN
    o_c = o_v + D_ATTN
    o_g = o_c + 3 * d_c

    xp = x_prompt.reshape(b * t, d)
    xs = x_sample.reshape(n, d)
    ckt = jnp.transpose(cache_k, (0, 1, 3, 4, 2)).reshape(depth, n_pool, D_ATTN, page)
    cvt = jnp.transpose(cache_v, (0, 1, 3, 4, 2)).reshape(depth, n_pool, D_ATTN, page)
    zero_a = jnp.zeros((b, -(-(wa - 1) // SUBLANES) * SUBLANES, d_a), F32)
    zero_c = jnp.zeros((b, -(-(wc - 1) // SUBLANES) * SUBLANES, d_c), F32)
    sta_pad = _pad_rows_front(state_conv_a, SUBLANES)
    stc_pad = _pad_rows_front(state_conv_c, SUBLANES)
    vec = lambda a: a.reshape(1, -1)

    outs = [[] for _ in range(8)]
    for l in range(depth):
        f1 = (vec(ffn1_pre_g[l]), vec(ffn1_post_g[l]), ffn1_w_gate[l].astype(BF16),
              ffn1_w_up[l].astype(BF16), ffn1_w_down[l].astype(BF16))
        f2 = (vec(ffn2_pre_g[l]), vec(ffn2_post_g[l]), ffn2_w_gate[l].astype(BF16),
              ffn2_w_up[l].astype(BF16), ffn2_w_down[l].astype(BF16))
        wl = w_in[l]
        proj_w = (vec(mix_pre_g[l]), wl[:, :o_q].astype(BF16), wl[:, o_q:o_k].T.astype(BF16),
                  wl[:, o_k:o_v].astype(BF16), wl[:, o_v:o_c].T.astype(BF16), wl[:, o_c:o_g].astype(BF16))
        conv = (conv_a_w[l], vec(conv_a_b[l]), vec(ln_a_g[l]), vec(ln_a_b[l]), conv_c_w[l])
        merge_w = (vec(mix_pre_g[l]), vec(mix_post_g[l]), wl[:, o_g:].astype(BF16), w_a_out[l].astype(BF16),
                   w_b_out[l].astype(BF16), w_c_out[l].astype(BF16), w_o[l].astype(BF16))
        bias_rows = jnp.broadcast_to(jnp.repeat(sb_bias[l], t_seq)[:, None], (N_HEADS * t_seq, LANES))

        xp = _ffn(xp, *f1, tile=ROW_TILE)
        qt, kb, kt, vt, vtb, sa, cp, na, nc = _mixproj_prompt(
            xp.reshape(b, t, d), *proj_w, zero_a, zero_c, *conv, tile=MIX_TILE)
        ot = _attn_prompt(sb_bias[l], qt, kb, vtb, tile=ATTN_TILE, heads=ATTN_HEADS)
        xp = _merge(xp.reshape(b, t, d), sa, ot, cp, *merge_w, tile=ROW_TILE, o_feature_major=True)
        xp = _ffn(xp.reshape(b * t, d), *f2, tile=ROW_TILE)
        for lst, val in zip(outs[:4], (kt, vt, na, nc)):
            lst.append(val)

        xs = _ffn(xs, *f1, tile=n)
        q, k, v, sa, cp, na, nc = _mixproj_sample(xs, *proj_w, sta_pad[l], stc_pad[l], *conv,
                                                  n_seq=n_seq, t_seq=t_seq)
        o = _attn_sample(page_table, q, k, v, bias_rows, ckt, cvt, l, t_seq, PAGES_PER_STEP)
        xs = _merge(xs[None], sa[None], o[None], cp[None], *merge_w, tile=n, o_feature_major=False)[0]
        xs = _ffn(xs, *f2, tile=n)
        for lst, val in zip(outs[4:], (k.reshape(n_seq, t_seq, N_HEADS, HEAD_DIM),
                                       v.reshape(n_seq, t_seq, N_HEADS, HEAD_DIM), na, nc)):
            lst.append(val)

    kv_out = lambda xs_: jnp.transpose(jnp.stack(xs_).reshape(depth, b, N_HEADS, HEAD_DIM, t), (0, 1, 4, 2, 3))
    return (xp.reshape(b, t, d), xs.reshape(n_seq, t_seq, d), kv_out(outs[0]), kv_out(outs[1]),
            *[jnp.stack(o) for o in outs[2:]])
```

```python
import functools

import jax
import jax.numpy as jnp
from jax import lax
from jax.experimental import pallas as pl
from jax.experimental.pallas import tpu as pltpu

EPS = 1e-6
N_HEADS = 16
HEAD_DIM = 64
D_ATTN = N_HEADS * HEAD_DIM
LANES = 128
SUBLANES = 8
VMEM_LIMIT = 56 * 1024 * 1024
SOFTPLUS_LINEAR_ABOVE = 60.0

F32 = jnp.float32
BF16 = jnp.bfloat16


def _params(n_grid):
    return pltpu.CompilerParams(dimension_semantics=("arbitrary",) * n_grid,
                                vmem_limit_bytes=VMEM_LIMIT)


def _resident(shape):
    zeros = (0,) * len(shape)
    return pl.BlockSpec(shape, lambda *_: zeros, pipeline_mode=pl.Buffered(1))


def _layer_of(stacked, layer):
    index = (layer,) + (0,) * (stacked.ndim - 1)
    return pl.BlockSpec((None,) + stacked.shape[1:], lambda *_: index, pipeline_mode=pl.Buffered(1))


def _rms(x, g):
    return x * lax.rsqrt(jnp.mean(x * x, axis=-1, keepdims=True) + EPS) * g


def _silu(x):
    return x * jax.nn.sigmoid(x)


def _softplus(z):
    return jnp.where(z > SOFTPLUS_LINEAR_ABOVE, z, jnp.log(1.0 + jnp.exp(z)))


def _dot(a, b):
    return jnp.dot(a, b, preferred_element_type=F32)


def _dot_nt(a, b):
    return lax.dot_general(a, b, (((1,), (1,)), ((), ())), preferred_element_type=F32)


def _dot_tn(a, b):
    return lax.dot_general(a, b, (((0,), (0,)), ((), ())), preferred_element_type=F32)


def _ffn_body(x_ref, pre_ref, post_ref, wg_ref, wu_ref, wd_ref, o_ref):
    x = x_ref[...]
    xn = _rms(x, pre_ref[...]).astype(BF16)
    h = (_silu(_dot(xn, wg_ref[...])) * _dot(xn, wu_ref[...])).astype(BF16)
    y = _dot(h, wd_ref[...])
    o_ref[...] = x + 0.5 * _rms(y, post_ref[...])


def _ffn(x, pre_g, post_g, wg, wu, wd, layer, tile):
    n, d = x.shape
    row = pl.BlockSpec((tile, d), lambda i: (i, 0))
    return pl.pallas_call(
        _ffn_body,
        grid=(n // tile,),
        in_specs=[row, _resident((1, d)), _resident((1, d)),
                  _layer_of(wg, layer), _layer_of(wu, layer), _layer_of(wd, layer)],
        out_specs=row,
        out_shape=jax.ShapeDtypeStruct((n, d), F32),
        compiler_params=_params(1),
        name="ffn",
    )(x, pre_g, post_g, wg, wu, wd)


def _proj_columns(d_a, d_c):
    o_q = 2 * d_a
    o_k = o_q + D_ATTN
    o_v = o_k + D_ATTN
    o_c = o_v + D_ATTN
    o_g = o_c + 3 * d_c
    return o_q, o_k, o_v, o_c, o_g


def _layernorm_silu(t, g, b):
    mu = jnp.mean(t, axis=-1, keepdims=True)
    c = t - mu
    var = jnp.mean(c * c, axis=-1, keepdims=True)
    return _silu(c * lax.rsqrt(var + EPS) * g + b)


def _conv_taps(w_ref, b, scr, phase_scr, first, rows, n_taps, lead=()):
    acc = b
    for phase in range(SUBLANES):
        taps = [j for j in range(n_taps) if (first + j) % SUBLANES == phase]
        if not taps:
            continue
        lo = first + taps[0]
        span = taps[-1] - taps[0] + rows
        if phase == 0:
            src, base = scr, lo
        else:
            phase_scr[lead + (slice(0, span), slice(None))] = scr[lead + (slice(lo, lo + span), slice(None))]
            src, base = phase_scr, 0
        for j in taps:
            off = base + j - taps[0]
            term = w_ref[j:j + 1, :] * src[lead + (slice(off, off + rows), slice(None))]
            acc = term if acc is None else acc + term
    return acc


def _mixproj_prompt_body(x_ref, pre_ref, w_ref, wqt_ref, wvt_ref, sta_ref, stc_ref,
                         caw_ref, cab_ref, lng_ref, lnb_ref, ccw_ref, *rest, tile, wa, wc, pad_a, pad_c):
    qt_ref, kb_ref, kt_ref, vt_ref, vtb_ref, sa_ref, cp_ref, na_ref, nc_ref, xa_scr, xc_scr, ph_scr = rest[-12:]
    i = pl.program_id(1)
    halo_a = wa - 1
    halo_c = wc - 1
    d_a = xa_scr.shape[1]
    d_c = xc_scr.shape[1]
    o_q, o_k, o_v, o_c, o_g = _proj_columns(d_a, d_c)

    @pl.when(i == 0)
    def _():
        xa_scr[0:pad_a, :] = sta_ref[...]
        xc_scr[0:pad_c, :] = stc_ref[...]

    @pl.when(i > 0)
    def _():
        xa_scr[pad_a - halo_a:pad_a, :] = xa_scr[tile + pad_a - halo_a:tile + pad_a, :]
        xc_scr[pad_c - halo_c:pad_c, :] = xc_scr[tile + pad_c - halo_c:tile + pad_c, :]

    u = _rms(x_ref[...], pre_ref[...]).astype(BF16)
    qt_ref[...] = (_dot_nt(wqt_ref[...], u) * (HEAD_DIM ** -0.5)).astype(BF16)
    k = _dot(u, w_ref[:, o_k:o_v])
    kb_ref[...] = k.astype(BF16)
    kt_ref[...] = k.T
    vt = _dot_nt(wvt_ref[...], u)
    vt_ref[...] = vt
    vtb_ref[...] = vt.astype(BF16)

    a2 = _dot(u, w_ref[:, 0:o_q])
    xa_scr[pad_a:pad_a + tile, :] = a2[:, :d_a] * jax.nn.sigmoid(a2[:, d_a:])
    conv = _conv_taps(caw_ref, cab_ref[...], xa_scr, ph_scr, pad_a - halo_a, tile, wa)
    sa_ref[...] = _layernorm_silu(conv, lng_ref[...], lnb_ref[...]).astype(BF16)

    c3 = _dot(u, w_ref[:, o_c:o_g])
    xc_scr[pad_c:pad_c + tile, :] = c3[:, 2 * d_c:] * c3[:, :d_c]
    cconv = _conv_taps(ccw_ref, None, xc_scr, ph_scr, pad_c - halo_c, tile, wc)
    cp_ref[...] = (c3[:, d_c:2 * d_c] * cconv).astype(BF16)

    @pl.when(i == pl.num_programs(1) - 1)
    def _():
        na_ref[...] = xa_scr[tile + pad_a - halo_a:tile + pad_a, :]
        nc_ref[...] = xc_scr[tile + pad_c - halo_c:tile + pad_c, :]


def _mixproj_prompt(x, pre_g, w_in, w_qt, w_vt, sta_pad, stc_pad, caw, cab, lng, lnb, ccw, kv_stacks, layer,
                    depth, tile):
    b, t, d = x.shape
    wa, d_a = caw.shape
    wc, d_c = ccw.shape
    pad_a, pad_c = sta_pad.shape[1], stc_pad.shape[1]
    rows = lambda w: pl.BlockSpec((None, tile, w), lambda bi, i: (bi, i, 0))
    cols = pl.BlockSpec((None, D_ATTN, tile), lambda bi, i: (bi, 0, i))
    layer_cols = pl.BlockSpec((None, None, D_ATTN, tile), lambda bi, i: (layer, bi, 0, i))
    per_seq = lambda r, w: pl.BlockSpec((None, r, w), lambda bi, i: (bi, 0, 0))
    full = lambda a: _resident(a.shape)
    body = functools.partial(_mixproj_prompt_body, tile=tile, wa=wa, wc=wc, pad_a=pad_a, pad_c=pad_c)
    args = [x, pre_g, w_in, w_qt, w_vt, sta_pad, stc_pad, caw, cab, lng, lnb, ccw]
    in_specs = [rows(d), full(pre_g), _layer_of(w_in, layer), full(w_qt), full(w_vt),
                per_seq(pad_a, d_a), per_seq(pad_c, d_c),
                full(caw), full(cab), full(lng), full(lnb), full(ccw)]
    aliases = {}
    if kv_stacks is not None:
        aliases = {len(args): 2, len(args) + 1: 3}
        args += list(kv_stacks)
        in_specs += [pl.BlockSpec(memory_space=pl.ANY)] * 2
    return pl.pallas_call(
        body,
        grid=(b, t // tile),
        in_specs=in_specs,
        out_specs=[cols, rows(D_ATTN), layer_cols, layer_cols,
                   pl.BlockSpec((None, None, D_ATTN, tile), lambda bi, i: (bi, i, 0, 0)),
                   rows(d_a), rows(d_c), per_seq(wa - 1, d_a), per_seq(wc - 1, d_c)],
        out_shape=[jax.ShapeDtypeStruct((b, D_ATTN, t), BF16),
                   jax.ShapeDtypeStruct((b, t, D_ATTN), BF16),
                   jax.ShapeDtypeStruct((depth, b, D_ATTN, t), F32),
                   jax.ShapeDtypeStruct((depth, b, D_ATTN, t), F32),
                   jax.ShapeDtypeStruct((b, t // tile, D_ATTN, tile), BF16),
                   jax.ShapeDtypeStruct((b, t, d_a), BF16),
                   jax.ShapeDtypeStruct((b, t, d_c), BF16),
                   jax.ShapeDtypeStruct((b, wa - 1, d_a), F32),
                   jax.ShapeDtypeStruct((b, wc - 1, d_c), F32)],
        scratch_shapes=[pltpu.VMEM((pad_a + tile, d_a), F32), pltpu.VMEM((pad_c + tile, d_c), F32),
                        pltpu.VMEM((pad_a + tile, max(d_a, d_c)), F32)],
        input_output_aliases=aliases,
        compiler_params=_params(2),
        name="mixproj_prompt",
    )(*args)


def _mixproj_sample_body(x_ref, pre_ref, w_ref, wqt_ref, wvt_ref, sta_ref, stc_ref,
                         caw_ref, cab_ref, lng_ref, lnb_ref, ccw_ref,
                         q_ref, k_ref, v_ref, sa_ref, cp_ref, na_ref, nc_ref,
                         xa_scr, xc_scr, ph_scr, *, n_seq, t_seq, wa, wc, pad_a, pad_c):
    halo_a = wa - 1
    halo_c = wc - 1
    d_a = xa_scr.shape[2]
    d_c = xc_scr.shape[2]
    n = n_seq * t_seq
    o_q, o_k, o_v, o_c, o_g = _proj_columns(d_a, d_c)
    u = _rms(x_ref[...], pre_ref[...]).astype(BF16)
    q_ref[...] = _dot_nt(u, wqt_ref[...]) * (HEAD_DIM ** -0.5)
    k_ref[...] = _dot(u, w_ref[:, o_k:o_v])
    v_ref[...] = _dot_nt(u, wvt_ref[...])

    a2 = _dot(u, w_ref[:, 0:o_q])
    xa_scr[:, 0:pad_a, :] = sta_ref[...]
    xa_scr[:, pad_a:pad_a + t_seq, :] = (a2[:, :d_a] * jax.nn.sigmoid(a2[:, d_a:])).reshape(n_seq, t_seq, d_a)
    conv = _conv_taps(caw_ref, cab_ref[...], xa_scr, ph_scr, pad_a - halo_a, t_seq, wa, lead=(slice(None),))
    sa_ref[...] = _layernorm_silu(conv, lng_ref[...], lnb_ref[...]).reshape(n, d_a).astype(BF16)

    c3 = _dot(u, w_ref[:, o_c:o_g])
    xc_scr[:, 0:pad_c, :] = stc_ref[...]
    xc_scr[:, pad_c:pad_c + t_seq, :] = (c3[:, 2 * d_c:] * c3[:, :d_c]).reshape(n_seq, t_seq, d_c)
    cconv = _conv_taps(ccw_ref, None, xc_scr, ph_scr, pad_c - halo_c, t_seq, wc, lead=(slice(None),))
    cp_ref[...] = (c3[:, d_c:2 * d_c] * cconv.reshape(n, d_c)).astype(BF16)

    na_ref[...] = xa_scr[:, pad_a + t_seq - halo_a:pad_a + t_seq, :]
    nc_ref[...] = xc_scr[:, pad_c + t_seq - halo_c:pad_c + t_seq, :]


def _mixproj_sample(x, pre_g, w_in, w_qt, w_vt, sta_pad, stc_pad, caw, cab, lng, lnb, ccw, layer,
                    n_seq, t_seq):
    n, d = x.shape
    wa, d_a = caw.shape
    wc, d_c = ccw.shape
    pad_a, pad_c = sta_pad.shape[1], stc_pad.shape[1]
    body = functools.partial(_mixproj_sample_body, n_seq=n_seq, t_seq=t_seq, wa=wa, wc=wc,
                             pad_a=pad_a, pad_c=pad_c)
    full = lambda a: _resident(a.shape)
    out_shape = [jax.ShapeDtypeStruct((n, D_ATTN), F32),
                 jax.ShapeDtypeStruct((n, D_ATTN), F32),
                 jax.ShapeDtypeStruct((n, D_ATTN), F32),
                 jax.ShapeDtypeStruct((n, d_a), BF16),
                 jax.ShapeDtypeStruct((n, d_c), BF16),
                 jax.ShapeDtypeStruct((n_seq, wa - 1, d_a), F32),
                 jax.ShapeDtypeStruct((n_seq, wc - 1, d_c), F32)]
    args = (x, pre_g, w_in, w_qt, w_vt, sta_pad, stc_pad, caw, cab, lng, lnb, ccw)
    return pl.pallas_call(
        body,
        grid=(1,),
        in_specs=[_layer_of(a, layer) if a is w_in else full(a) for a in args],
        out_specs=[pl.BlockSpec(s.shape, lambda i, nd=len(s.shape): (0,) * nd) for s in out_shape],
        out_shape=out_shape,
        scratch_shapes=[pltpu.VMEM((n_seq, pad_a + t_seq, d_a), F32),
                        pltpu.VMEM((n_seq, pad_c + t_seq, d_c), F32),
                        pltpu.VMEM((n_seq, pad_a + t_seq, max(d_a, d_c)), F32)],
        compiler_params=_params(1),
        name="mixproj_sample",
    )(*args)


def _attn_prompt_body(bias_ref, qt_ref, k_ref, vt_ref, tri_ref, ot_ref, z_scr, a_scr, *, tile, heads):
    g = pl.program_id(1)
    i = pl.program_id(2)
    tri = tri_ref[...]
    qt = qt_ref[...]
    zero = jnp.zeros((HEAD_DIM, tile), BF16)
    pair = LANES // HEAD_DIM
    qm = []
    for h in range(heads):
        parts = [zero] * pair
        parts[h % pair] = qt[h * HEAD_DIM:(h + 1) * HEAD_DIM, :]
        qm.append(jnp.concatenate(parts, axis=0))
    ones = jnp.ones((tile, LANES), BF16)
    row = lax.broadcasted_iota(jnp.int32, (LANES, tile), 0)
    qe = []
    for h in range(heads):
        b = jnp.full((LANES, tile), bias_ref[g * heads + h], F32)
        hi = b.astype(BF16).astype(F32)
        mid = (b - hi).astype(BF16).astype(F32)
        lo = b - hi - mid
        piece = jnp.where(row == 0, hi, jnp.where(row == 1, mid, jnp.where(row == 2, lo, 0.0)))
        qe.append(jnp.concatenate([qm[h], piece.astype(BF16)], axis=0))
    key_before_query = (lax.broadcasted_iota(jnp.int32, (tile, tile), 0)
                        < lax.broadcasted_iota(jnp.int32, (tile, tile), 1))

    def logits(j):
        kt = k_ref[pl.ds(pl.multiple_of(j * tile, tile), tile), :]
        return [_dot(jnp.concatenate([kt[:, (h // pair) * LANES:(h // pair + 1) * LANES], ones], axis=1), qe[h])
                for h in range(heads)]

    def weights(zs, carries, mask):
        sps = []
        for h in range(heads):
            sp = _softplus(zs[h])
            if mask is not None:
                sp = jnp.where(mask, sp, 0.0)
            sps.append(sp.astype(BF16))
        incls = [_dot(tri, sps[h]) for h in range(heads)]
        ws = []
        for h in range(heads):
            a = jnp.exp(zs[h] - incls[h] - carries[h])
            if mask is not None:
                a = jnp.where(mask, a, 0.0)
            ws.append(a.astype(BF16))
        return ws, [carries[h] + incls[h][0:1, :] for h in range(heads)]

    def values(j, ws, accs):
        vt = vt_ref[j]
        return [accs[h] + _dot(vt[h * HEAD_DIM:(h + 1) * HEAD_DIM, :], ws[h]) for h in range(heads)]

    carries = [jnp.zeros((1, tile), F32)] * heads
    accs = [jnp.zeros((HEAD_DIM, tile), F32)] * heads

    def load(scr, slot):
        return [scr[slot, h] for h in range(heads)]

    def store(scr, slot, xs):
        for h in range(heads):
            scr[slot, h] = xs[h]

    def stage(j, slot, carries, accs, look_ahead=True):
        accs = values(j + 1, load(a_scr, 1 - slot), accs)
        if look_ahead:
            store(z_scr, 1 - slot, logits(jnp.maximum(j - 1, 0)))
        ws, carries = weights(load(z_scr, slot), carries, None)
        store(a_scr, slot, ws)
        return carries, accs

    z_diag = logits(i)
    store(z_scr, 0, logits(jnp.maximum(i - 1, 0)))
    ws, carries = weights(z_diag, carries, key_before_query)
    store(a_scr, 1, ws)

    def trip(s, state):
        j = i - 1 - 2 * s
        carries, accs = stage(j, 0, *state)
        return stage(j - 1, 1, carries, accs)

    carries, accs = lax.fori_loop(0, i // 2, trip, (carries, accs))

    def odd_tail(carries, accs):
        _, accs = stage(0, 0, carries, accs, look_ahead=False)
        return values(0, load(a_scr, 0), accs)

    accs = lax.cond(i % 2 == 1, odd_tail, lambda carries, accs: values(0, load(a_scr, 1), accs), carries, accs)
    ot_ref[...] = jnp.concatenate(accs, axis=0).astype(BF16)


def _attn_prompt(bias, qt, k, vtb, tile, heads):
    b, d, t = qt.shape
    rows = heads * HEAD_DIM
    n_tiles = t // tile
    tri = (jnp.arange(tile)[None, :] >= jnp.arange(tile)[:, None]).astype(BF16)
    return pl.pallas_call(
        functools.partial(_attn_prompt_body, tile=tile, heads=heads),
        grid=(b, d // rows, n_tiles),
        in_specs=[pl.BlockSpec(memory_space=pltpu.SMEM),
                  pl.BlockSpec((None, rows, tile), lambda bi, g, i: (bi, g, i)),
                  pl.BlockSpec((None, t, rows), lambda bi, g, i: (bi, 0, g)),
                  pl.BlockSpec((None, n_tiles, rows, tile), lambda bi, g, i: (bi, 0, g, 0)),
                  _resident(tri.shape)],
        out_specs=pl.BlockSpec((None, rows, tile), lambda bi, g, i: (bi, g, i)),
        out_shape=jax.ShapeDtypeStruct((b, d, t), BF16),
        scratch_shapes=[pltpu.VMEM((2, heads, tile, tile), F32), pltpu.VMEM((2, heads, tile, tile), BF16)],
        compiler_params=_params(3),
        name="attn_prompt",
    )(bias, qt, k, vtb, tri)


def _attn_sample_body(pt_ref, q_ref, kn_ref, vn_ref, bias_ref, tri_ref, *rest, t_seq, pages_per_step):
    kp_refs = rest[:pages_per_step]
    vp_refs = rest[pages_per_step:2 * pages_per_step]
    o_ref = rest[2 * pages_per_step]
    qbd_scr, carry_scr, acc_scr = rest[2 * pages_per_step + 1:]
    g = pl.program_id(1)
    rows = N_HEADS * t_seq
    d = acc_scr.shape[1]
    page = kp_refs[0].shape[1]
    own_head = (lax.broadcasted_iota(jnp.int32, (rows, d), 0) // t_seq
                == lax.broadcasted_iota(jnp.int32, (rows, d), 1) // HEAD_DIM)
    tri = tri_ref[...]

    def weights(z, carry, mask):
        sp = _softplus(z)
        if mask is not None:
            sp = jnp.where(mask, sp, 0.0)
        sp = sp.astype(BF16)
        ws = []
        for p in range(z.shape[1] // page):
            cols = slice(p * page, (p + 1) * page)
            r = _dot(sp[:, cols], tri)
            a = jnp.exp(z[:, cols] - r[:, :page] - carry)
            if mask is not None:
                a = jnp.where(mask, a, 0.0)
            ws.append(a.astype(BF16))
            carry = carry + r[:, page:]
        return (ws[0] if len(ws) == 1 else jnp.concatenate(ws, axis=1)), carry

    @pl.when(g == 0)
    def _():
        qt = jnp.concatenate([q_ref[...]] * N_HEADS, axis=0)
        qbd = jnp.where(own_head, qt, 0.0).astype(BF16)
        qbd_scr[...] = qbd
        pad = jnp.zeros((page - t_seq, d), F32)
        kc = jnp.concatenate([kn_ref[...], pad], axis=0).astype(BF16)
        vc = jnp.concatenate([vn_ref[...], pad], axis=0).astype(BF16)
        key_idx = lax.broadcasted_iota(jnp.int32, (rows, page), 1)
        qry_idx = lax.broadcasted_iota(jnp.int32, (rows, page), 0) % t_seq
        w, carry = weights(_dot_nt(qbd, kc) + bias_ref[...], jnp.zeros((rows, LANES), F32), key_idx < qry_idx)
        acc_scr[...] = _dot(w, vc)
        carry_scr[...] = carry

    k_all = jnp.concatenate([r[...].astype(BF16) for r in kp_refs], axis=1)
    v_all = jnp.concatenate([r[...].astype(BF16) for r in vp_refs], axis=1)
    bias_all = jnp.concatenate([bias_ref[...]] * pages_per_step, axis=1)
    w, carry = weights(_dot(qbd_scr[...], k_all) + bias_all, carry_scr[...], None)
    acc_scr[...] += _dot_nt(w, v_all)
    carry_scr[...] = carry

    @pl.when(g == pl.num_programs(1) - 1)
    def _():
        m = jnp.where(own_head, acc_scr[...], 0.0)
        out = m[0:t_seq, :]
        for h in range(1, N_HEADS):
            out = out + m[h * t_seq:(h + 1) * t_seq, :]
        o_ref[...] = out


def _attn_sample(page_table, q, k_new, v_new, bias_rows, cache_kt, cache_vt, layer, t_seq, pages_per_step):
    n, d = q.shape
    n_seq, n_pages = page_table.shape
    page = cache_kt.shape[3]
    rows = N_HEADS * t_seq
    tri = jnp.concatenate([(jnp.arange(page)[:, None] >= jnp.arange(page)[None, :]).astype(BF16),
                           jnp.ones((page, LANES), BF16)], axis=1)
    seq_rows = pl.BlockSpec((t_seq, d), lambda b, g, pt: (b, 0))

    def page_spec(p):
        def index(b, g, pt):
            return (layer, pt[b * n_pages + (n_pages - 1 - (g * pages_per_step + p))], 0, 0)
        return pl.BlockSpec((None, None, d, page), index)

    const = lambda a: pl.BlockSpec(a.shape, lambda b, g, pt: (0,) * a.ndim)
    grid_spec = pltpu.PrefetchScalarGridSpec(
        num_scalar_prefetch=1,
        grid=(n_seq, n_pages // pages_per_step),
        in_specs=[seq_rows, seq_rows, seq_rows, const(bias_rows), const(tri)]
                 + [page_spec(p) for p in range(pages_per_step)] * 2,
        out_specs=seq_rows,
        scratch_shapes=[pltpu.VMEM((rows, d), BF16), pltpu.VMEM((rows, LANES), F32),
                        pltpu.VMEM((rows, d), F32)],
    )
    return pl.pallas_call(
        functools.partial(_attn_sample_body, t_seq=t_seq, pages_per_step=pages_per_step),
        grid_spec=grid_spec,
        out_shape=jax.ShapeDtypeStruct((n, d), F32),
        compiler_params=_params(2),
        name="attn_sample",
    )(page_table.reshape(-1), q, k_new, v_new, bias_rows, tri,
      *([cache_kt] * pages_per_step), *([cache_vt] * pages_per_step))


def _merge_body(x_ref, sa_ref, o_ref, cp_ref, pre_ref, post_ref, win_ref, wa_ref, wb_ref, wc_ref,
                wo_ref, y_ref, *, o_feature_major):
    x = x_ref[...]
    d = x.shape[-1]
    o_g = _proj_columns(sa_ref.shape[-1], cp_ref.shape[-1])[-1]
    wgate_ref = win_ref.at[:, o_g:o_g + 3 * d]
    u = _rms(x, pre_ref[...]).astype(BF16)
    if o_feature_major:
        branch_b = _dot_tn(o_ref[...], wb_ref[...])
    else:
        branch_b = _dot(o_ref[...].astype(BF16), wb_ref[...])
    merged = jax.nn.sigmoid(_dot(u, wgate_ref[:, 0:d])) * _dot(sa_ref[...], wa_ref[...])
    merged = merged + jax.nn.sigmoid(_dot(u, wgate_ref[:, d:2 * d])) * branch_b
    merged = merged + jax.nn.sigmoid(_dot(u, wgate_ref[:, 2 * d:3 * d])) * _dot(cp_ref[...], wc_ref[...])
    m = _dot(merged.astype(BF16), wo_ref[...])
    y_ref[...] = x + _rms(m, post_ref[...])


def _merge(x, sa, o, cp, pre_g, post_g, w_in, w_a, w_b, w_c, w_o, layer, tile, o_feature_major):
    b, t, d = x.shape
    rows = lambda w: pl.BlockSpec((None, tile, w), lambda bi, i: (bi, i, 0))
    o_spec = (pl.BlockSpec((None, o.shape[1], tile), lambda bi, i: (bi, 0, i)) if o_feature_major
              else rows(o.shape[2]))
    full = lambda a: _resident(a.shape)
    return pl.pallas_call(
        functools.partial(_merge_body, o_feature_major=o_feature_major),
        grid=(b, t // tile),
        in_specs=[rows(d), rows(sa.shape[2]), o_spec, rows(cp.shape[2]),
                  full(pre_g), full(post_g)] + [_layer_of(w, layer) for w in (w_in, w_a, w_b, w_c, w_o)],
        out_specs=rows(d),
        out_shape=jax.ShapeDtypeStruct((b, t, d), F32),
        compiler_params=_params(2),
        name="merge",
    )(x, sa, o, cp, pre_g, post_g, w_in, w_a, w_b, w_c, w_o)


ROW_TILE = 512
MIX_TILE = 256
ATTN_TILE = 256
ATTN_HEADS = 4
PAGES_PER_STEP = 16


def _pad_rows_front(a, multiple):
    r = a.shape[-2]
    pad = -r % multiple
    return jnp.pad(a, [(0, 0)] * (a.ndim - 2) + [(pad, 0), (0, 0)])


def kernel(x_prompt, x_sample, cache_k, cache_v, state_conv_a, state_conv_c, page_table, ffn1_pre_g, ffn1_post_g, ffn1_w_gate, ffn1_w_up, ffn1_w_down, mix_pre_g, mix_post_g, w_in, conv_a_w, conv_a_b, ln_a_g, ln_a_b, w_a_out, w_b_out, sb_bias, conv_c_w, w_c_out, w_o, ffn2_pre_g, ffn2_post_g, ffn2_w_gate, ffn2_w_up, ffn2_w_down):
    b, t, d = x_prompt.shape
    n_seq, t_seq, _ = x_sample.shape
    depth, n_pool, page = cache_k.shape[:3]
    wa, d_a = conv_a_w.shape[1:]
    wc, d_c = conv_c_w.shape[1:]
    n = n_seq * t_seq
    o_q, o_k, o_v, o_c, o_g = _proj_columns(d_a, d_c)

    xp = x_prompt.reshape(b * t, d)
    xs = x_sample.reshape(n, d)
    ckt = jnp.transpose(cache_k, (0, 1, 3, 4, 2)).reshape(depth, n_pool, D_ATTN, page)
    cvt = jnp.transpose(cache_v, (0, 1, 3, 4, 2)).reshape(depth, n_pool, D_ATTN, page)
    zero_a = jnp.zeros((b, -(-(wa - 1) // SUBLANES) * SUBLANES, d_a), F32)
    zero_c = jnp.zeros((b, -(-(wc - 1) // SUBLANES) * SUBLANES, d_c), F32)
    sta_pad = _pad_rows_front(state_conv_a, SUBLANES)
    stc_pad = _pad_rows_front(state_conv_c, SUBLANES)
    vec = lambda a: a.reshape(1, -1)
    f1_w = [w.astype(BF16) for w in (ffn1_w_gate, ffn1_w_up, ffn1_w_down)]
    f2_w = [w.astype(BF16) for w in (ffn2_w_gate, ffn2_w_up, ffn2_w_down)]
    w_in_b = w_in.astype(BF16)
    out_w = [w.astype(BF16) for w in (w_a_out, w_b_out, w_c_out, w_o)]

    kv_stacks = None
    outs = [[] for _ in range(6)]
    for l in range(depth):
        f1 = (vec(ffn1_pre_g[l]), vec(ffn1_post_g[l]), *f1_w, l)
        f2 = (vec(ffn2_pre_g[l]), vec(ffn2_post_g[l]), *f2_w, l)
        proj_w = (vec(mix_pre_g[l]), w_in_b, w_in_b[l, :, o_q:o_k].T, w_in_b[l, :, o_v:o_c].T)
        conv = (conv_a_w[l], vec(conv_a_b[l]), vec(ln_a_g[l]), vec(ln_a_b[l]), conv_c_w[l])
        merge_w = (vec(mix_pre_g[l]), vec(mix_post_g[l]), w_in_b, *out_w, l)
        bias_rows = jnp.broadcast_to(jnp.repeat(sb_bias[l], t_seq)[:, None], (N_HEADS * t_seq, LANES))

        xp = _ffn(xp, *f1, tile=ROW_TILE)
        qt, kb, kt_all, vt_all, vtb, sa, cp, na, nc = _mixproj_prompt(
            xp.reshape(b, t, d), *proj_w, zero_a, zero_c, *conv, kv_stacks, l, depth, tile=MIX_TILE)
        kv_stacks = (kt_all, vt_all)
        ot = _attn_prompt(sb_bias[l], qt, kb, vtb, tile=ATTN_TILE, heads=ATTN_HEADS)
        xp = _merge(xp.reshape(b, t, d), sa, ot, cp, *merge_w, tile=ROW_TILE, o_feature_major=True)
        xp = _ffn(xp.reshape(b * t, d), *f2, tile=ROW_TILE)
        outs[0].append(na)
        outs[1].append(nc)

        xs = _ffn(xs, *f1, tile=n)
        q, k, v, sa, cp, na, nc = _mixproj_sample(xs, *proj_w, sta_pad[l], stc_pad[l], *conv, l,
                                                  n_seq=n_seq, t_seq=t_seq)
        o = _attn_sample(page_table, q, k, v, bias_rows, ckt, cvt, l, t_seq, PAGES_PER_STEP)
        xs = _merge(xs[None], sa[None], o[None], cp[None], *merge_w, tile=n, o_feature_major=False)[0]
        xs = _ffn(xs, *f2, tile=n)
        for lst, val in zip(outs[2:], (k.reshape(n_seq, t_seq, N_HEADS, HEAD_DIM),
                                       v.reshape(n_seq, t_seq, N_HEADS, HEAD_DIM), na, nc)):
            lst.append(val)

    kv_out = lambda a: jnp.transpose(a.reshape(depth, b, N_HEADS, HEAD_DIM, t), (0, 1, 4, 2, 3))
    return (xp.reshape(b, t, d), xs.reshape(n_seq, t_seq, d), kv_out(kv_stacks[0]), kv_out(kv_stacks[1]),
            *[jnp.stack(o) for o in outs])
```

```python
import functools

import jax
import jax.numpy as jnp
from jax import lax
from jax.experimental import pallas as pl
from jax.experimental.pallas import tpu as pltpu

EPS = 1e-6
N_HEADS = 16
HEAD_DIM = 64
D_ATTN = N_HEADS * HEAD_DIM
LANES = 128
SUBLANES = 8
VMEM_LIMIT = 56 * 1024 * 1024
SOFTPLUS_LINEAR_ABOVE = 60.0

F32 = jnp.float32
BF16 = jnp.bfloat16


def _params(n_grid):
    return pltpu.CompilerParams(dimension_semantics=("arbitrary",) * n_grid,
                                vmem_limit_bytes=VMEM_LIMIT)


def _resident(shape):
    zeros = (0,) * len(shape)
    return pl.BlockSpec(shape, lambda *_: zeros, pipeline_mode=pl.Buffered(1))


def _layer_of(stacked, layer):
    index = (layer,) + (0,) * (stacked.ndim - 1)
    return pl.BlockSpec((None,) + stacked.shape[1:], lambda *_: index, pipeline_mode=pl.Buffered(1))


def _rms(x, g):
    return x * lax.rsqrt(jnp.mean(x * x, axis=-1, keepdims=True) + EPS) * g


def _silu(x):
    return x * jax.nn.sigmoid(x)


def _softplus(z):
    return jnp.where(z > SOFTPLUS_LINEAR_ABOVE, z, jnp.log(1.0 + jnp.exp(z)))


def _dot(a, b):
    return jnp.dot(a, b, preferred_element_type=F32)


def _dot_nt(a, b):
    return lax.dot_general(a, b, (((1,), (1,)), ((), ())), preferred_element_type=F32)


def _dot_tn(a, b):
    return lax.dot_general(a, b, (((0,), (0,)), ((), ())), preferred_element_type=F32)


def _ffn_body(x_ref, pre_ref, post_ref, wg_ref, wu_ref, wd_ref, o_ref):
    x = x_ref[...]
    xn = _rms(x, pre_ref[...]).astype(BF16)
    h = (_silu(_dot(xn, wg_ref[...])) * _dot(xn, wu_ref[...])).astype(BF16)
    y = _dot(h, wd_ref[...])
    o_ref[...] = x + 0.5 * _rms(y, post_ref[...])


def _ffn(x, pre_g, post_g, wg, wu, wd, layer, tile):
    n, d = x.shape
    row = pl.BlockSpec((tile, d), lambda i: (i, 0))
    return pl.pallas_call(
        _ffn_body,
        grid=(n // tile,),
        in_specs=[row, _resident((1, d)), _resident((1, d)),
                  _layer_of(wg, layer), _layer_of(wu, layer), _layer_of(wd, layer)],
        out_specs=row,
        out_shape=jax.ShapeDtypeStruct((n, d), F32),
        compiler_params=_params(1),
        name="ffn",
    )(x, pre_g, post_g, wg, wu, wd)


def _proj_columns(d_a, d_c):
    o_q = 2 * d_a
    o_k = o_q + D_ATTN
    o_v = o_k + D_ATTN
    o_c = o_v + D_ATTN
    o_g = o_c + 3 * d_c
    return o_q, o_k, o_v, o_c, o_g


def _layernorm_silu(t, g, b):
    mu = jnp.mean(t, axis=-1, keepdims=True)
    c = t - mu
    var = jnp.mean(c * c, axis=-1, keepdims=True)
    return _silu(c * lax.rsqrt(var + EPS) * g + b)


def _conv_taps(w_ref, b, scr, phase_scr, first, rows, n_taps, lead=()):
    acc = b
    for phase in range(SUBLANES):
        taps = [j for j in range(n_taps) if (first + j) % SUBLANES == phase]
        if not taps:
            continue
        lo = first + taps[0]
        span = taps[-1] - taps[0] + rows
        if phase == 0:
            src, base = scr, lo
        else:
            phase_scr[lead + (slice(0, span), slice(None))] = scr[lead + (slice(lo, lo + span), slice(None))]
            src, base = phase_scr, 0
        for j in taps:
            off = base + j - taps[0]
            term = w_ref[j:j + 1, :] * src[lead + (slice(off, off + rows), slice(None))]
            acc = term if acc is None else acc + term
    return acc


def _mixproj_prompt_body(x_ref, pre_ref, w_ref, wqt_ref, wvt_ref, sta_ref, stc_ref,
                         caw_ref, cab_ref, lng_ref, lnb_ref, ccw_ref, *rest, tile, wa, wc, pad_a, pad_c):
    qt_ref, kb_ref, kt_ref, vt_ref, vtb_ref, sa_ref, cp_ref, na_ref, nc_ref, xa_scr, xc_scr, ph_scr = rest[-12:]
    i = pl.program_id(1)
    halo_a = wa - 1
    halo_c = wc - 1
    d_a = xa_scr.shape[1]
    d_c = xc_scr.shape[1]
    o_q, o_k, o_v, o_c, o_g = _proj_columns(d_a, d_c)

    @pl.when(i == 0)
    def _():
        xa_scr[0:pad_a, :] = sta_ref[...]
        xc_scr[0:pad_c, :] = stc_ref[...]

    @pl.when(i > 0)
    def _():
        xa_scr[pad_a - halo_a:pad_a, :] = xa_scr[tile + pad_a - halo_a:tile + pad_a, :]
        xc_scr[pad_c - halo_c:pad_c, :] = xc_scr[tile + pad_c - halo_c:tile + pad_c, :]

    u = _rms(x_ref[...], pre_ref[...]).astype(BF16)
    qt_ref[...] = (_dot_nt(wqt_ref[...], u) * (HEAD_DIM ** -0.5)).astype(BF16)
    k = _dot(u, w_ref[:, o_k:o_v])
    kb_ref[...] = k.astype(BF16)
    kt_ref[...] = k.T
    vt = _dot_nt(wvt_ref[...], u)
    vt_ref[...] = vt
    vtb_ref[...] = vt.astype(BF16)

    a2 = _dot(u, w_ref[:, 0:o_q])
    xa_scr[pad_a:pad_a + tile, :] = a2[:, :d_a] * jax.nn.sigmoid(a2[:, d_a:])
    conv = _conv_taps(caw_ref, cab_ref[...], xa_scr, ph_scr, pad_a - halo_a, tile, wa)
    sa_ref[...] = _layernorm_silu(conv, lng_ref[...], lnb_ref[...]).astype(BF16)

    c3 = _dot(u, w_ref[:, o_c:o_g])
    xc_scr[pad_c:pad_c + tile, :] = c3[:, 2 * d_c:] * c3[:, :d_c]
    cconv = _conv_taps(ccw_ref, None, xc_scr, ph_scr, pad_c - halo_c, tile, wc)
    cp_ref[...] = (c3[:, d_c:2 * d_c] * cconv).astype(BF16)

    @pl.when(i == pl.num_programs(1) - 1)
    def _():
        na_ref[...] = xa_scr[tile + pad_a - halo_a:tile + pad_a, :]
        nc_ref[...] = xc_scr[tile + pad_c - halo_c:tile + pad_c, :]


def _mixproj_prompt(x, pre_g, w_in, w_qt, w_vt, sta_pad, stc_pad, caw, cab, lng, lnb, ccw, kv_stacks, layer,
                    depth, tile):
    b, t, d = x.shape
    wa, d_a = caw.shape
    wc, d_c = ccw.shape
    pad_a, pad_c = sta_pad.shape[1], stc_pad.shape[1]
    rows = lambda w: pl.BlockSpec((None, tile, w), lambda bi, i: (bi, i, 0))
    cols = pl.BlockSpec((None, D_ATTN, tile), lambda bi, i: (bi, 0, i))
    layer_cols = pl.BlockSpec((None, None, D_ATTN, tile), lambda bi, i: (layer, bi, 0, i))
    per_seq = lambda r, w: pl.BlockSpec((None, r, w), lambda bi, i: (bi, 0, 0))
    full = lambda a: _resident(a.shape)
    body = functools.partial(_mixproj_prompt_body, tile=tile, wa=wa, wc=wc, pad_a=pad_a, pad_c=pad_c)
    args = [x, pre_g, w_in, w_qt, w_vt, sta_pad, stc_pad, caw, cab, lng, lnb, ccw]
    in_specs = [rows(d), full(pre_g), _layer_of(w_in, layer), full(w_qt), full(w_vt),
                per_seq(pad_a, d_a), per_seq(pad_c, d_c),
                full(caw), full(cab), full(lng), full(lnb), full(ccw)]
    aliases = {}
    if kv_stacks is not None:
        aliases = {len(args): 2, len(args) + 1: 3}
        args += list(kv_stacks)
        in_specs += [pl.BlockSpec(memory_space=pl.ANY)] * 2
    return pl.pallas_call(
        body,
        grid=(b, t // tile),
        in_specs=in_specs,
        out_specs=[cols, rows(D_ATTN), layer_cols, layer_cols,
                   pl.BlockSpec((None, None, D_ATTN, tile), lambda bi, i: (bi, i, 0, 0)),
                   rows(d_a), rows(d_c), per_seq(wa - 1, d_a), per_seq(wc - 1, d_c)],
        out_shape=[jax.ShapeDtypeStruct((b, D_ATTN, t), BF16),
                   jax.ShapeDtypeStruct((b, t, D_ATTN), BF16),
                   jax.ShapeDtypeStruct((depth, b, D_ATTN, t), F32),
                   jax.ShapeDtypeStruct((depth, b, D_ATTN, t), F32),
                   jax.ShapeDtypeStruct((b, t // tile, D_ATTN, tile), BF16),
                   jax.ShapeDtypeStruct((b, t, d_a), BF16),
                   jax.ShapeDtypeStruct((b, t, d_c), BF16),
                   jax.ShapeDtypeStruct((b, wa - 1, d_a), F32),
                   jax.ShapeDtypeStruct((b, wc - 1, d_c), F32)],
        scratch_shapes=[pltpu.VMEM((pad_a + tile, d_a), F32), pltpu.VMEM((pad_c + tile, d_c), F32),
                        pltpu.VMEM((pad_a + tile, max(d_a, d_c)), F32)],
        input_output_aliases=aliases,
        compiler_params=_params(2),
        name="mixproj_prompt",
    )(*args)


def _mixproj_sample_body(x_ref, pre_ref, w_ref, wqt_ref, wvt_ref, sta_ref, stc_ref,
                         caw_ref, cab_ref, lng_ref, lnb_ref, ccw_ref,
                         q_ref, k_ref, v_ref, sa_ref, cp_ref, na_ref, nc_ref,
                         xa_scr, xc_scr, ph_scr, *, n_seq, t_seq, wa, wc, pad_a, pad_c):
    halo_a = wa - 1
    halo_c = wc - 1
    d_a = xa_scr.shape[2]
    d_c = xc_scr.shape[2]
    n = n_seq * t_seq
    o_q, o_k, o_v, o_c, o_g = _proj_columns(d_a, d_c)
    u = _rms(x_ref[...], pre_ref[...]).astype(BF16)
    q_ref[...] = _dot_nt(u, wqt_ref[...]) * (HEAD_DIM ** -0.5)
    k_ref[...] = _dot(u, w_ref[:, o_k:o_v])
    v_ref[...] = _dot_nt(u, wvt_ref[...])

    a2 = _dot(u, w_ref[:, 0:o_q])
    xa_scr[:, 0:pad_a, :] = sta_ref[...]
    xa_scr[:, pad_a:pad_a + t_seq, :] = (a2[:, :d_a] * jax.nn.sigmoid(a2[:, d_a:])).reshape(n_seq, t_seq, d_a)
    conv = _conv_taps(caw_ref, cab_ref[...], xa_scr, ph_scr, pad_a - halo_a, t_seq, wa, lead=(slice(None),))
    sa_ref[...] = _layernorm_silu(conv, lng_ref[...], lnb_ref[...]).reshape(n, d_a).astype(BF16)

    c3 = _dot(u, w_ref[:, o_c:o_g])
    xc_scr[:, 0:pad_c, :] = stc_ref[...]
    xc_scr[:, pad_c:pad_c + t_seq, :] = (c3[:, 2 * d_c:] * c3[:, :d_c]).reshape(n_seq, t_seq, d_c)
    cconv = _conv_taps(ccw_ref, None, xc_scr, ph_scr, pad_c - halo_c, t_seq, wc, lead=(slice(None),))
    cp_ref[...] = (c3[:, d_c:2 * d_c] * cconv.reshape(n, d_c)).astype(BF16)

    na_ref[...] = xa_scr[:, pad_a + t_seq - halo_a:pad_a + t_seq, :]
    nc_ref[...] = xc_scr[:, pad_c + t_seq - halo_c:pad_c + t_seq, :]


def _mixproj_sample(x, pre_g, w_in, w_qt, w_vt, sta_pad, stc_pad, caw, cab, lng, lnb, ccw, layer,
                    n_seq, t_seq):
    n, d = x.shape
    wa, d_a = caw.shape
    wc, d_c = ccw.shape
    pad_a, pad_c = sta_pad.shape[1], stc_pad.shape[1]
    body = functools.partial(_mixproj_sample_body, n_seq=n_seq, t_seq=t_seq, wa=wa, wc=wc,
                             pad_a=pad_a, pad_c=pad_c)
    full = lambda a: _resident(a.shape)
    out_shape = [jax.ShapeDtypeStruct((n, D_ATTN), F32),
                 jax.ShapeDtypeStruct((n, D_ATTN), F32),
                 jax.ShapeDtypeStruct((n, D_ATTN), F32),
                 jax.ShapeDtypeStruct((n, d_a), BF16),
                 jax.ShapeDtypeStruct((n, d_c), BF16),
                 jax.ShapeDtypeStruct((n_seq, wa - 1, d_a), F32),
                 jax.ShapeDtypeStruct((n_seq, wc - 1, d_c), F32)]
    args = (x, pre_g, w_in, w_qt, w_vt, sta_pad, stc_pad, caw, cab, lng, lnb, ccw)
    return pl.pallas_call(
        body,
        grid=(1,),
        in_specs=[_layer_of(a, layer) if a is w_in else full(a) for a in args],
        out_specs=[pl.BlockSpec(s.shape, lambda i, nd=len(s.shape): (0,) * nd) for s in out_shape],
        out_shape=out_shape,
        scratch_shapes=[pltpu.VMEM((n_seq, pad_a + t_seq, d_a), F32),
                        pltpu.VMEM((n_seq, pad_c + t_seq, d_c), F32),
                        pltpu.VMEM((n_seq, pad_a + t_seq, max(d_a, d_c)), F32)],
        compiler_params=_params(1),
        name="mixproj_sample",
    )(*args)


def _attn_prompt_part(g, i, bias_ref, qt_ref, k_ref, vt_ref, tri_ref, ot_ref, z_scr, a_scr, tile, heads):
    tri = tri_ref[...]
    qt = qt_ref[...]
    zero = jnp.zeros((HEAD_DIM, tile), BF16)
    pair = LANES // HEAD_DIM
    qm = []
    for h in range(heads):
        parts = [zero] * pair
        parts[h % pair] = qt[h * HEAD_DIM:(h + 1) * HEAD_DIM, :]
        qm.append(jnp.concatenate(parts, axis=0))
    ones = jnp.ones((tile, LANES), BF16)
    row = lax.broadcasted_iota(jnp.int32, (LANES, tile), 0)
    qe = []
    for h in range(heads):
        b = jnp.full((LANES, tile), bias_ref[g * heads + h], F32)
        hi = b.astype(BF16).astype(F32)
        mid = (b - hi).astype(BF16).astype(F32)
        lo = b - hi - mid
        piece = jnp.where(row == 0, hi, jnp.where(row == 1, mid, jnp.where(row == 2, lo, 0.0)))
        qe.append(jnp.concatenate([qm[h], piece.astype(BF16)], axis=0))
    key_before_query = (lax.broadcasted_iota(jnp.int32, (tile, tile), 0)
                        < lax.broadcasted_iota(jnp.int32, (tile, tile), 1))

    def logits(j):
        kt = k_ref[pl.ds(pl.multiple_of(j * tile, tile), tile), :]
        return [_dot(jnp.concatenate([kt[:, (h // pair) * LANES:(h // pair + 1) * LANES], ones], axis=1), qe[h])
                for h in range(heads)]

    def weights(zs, carries, mask):
        sps = []
        for h in range(heads):
            sp = _softplus(zs[h])
            if mask is not None:
                sp = jnp.where(mask, sp, 0.0)
            sps.append(sp.astype(BF16))
        incls = [_dot(tri, sps[h]) for h in range(heads)]
        ws = []
        for h in range(heads):
            a = jnp.exp(zs[h] - incls[h] - carries[h])
            if mask is not None:
                a = jnp.where(mask, a, 0.0)
            ws.append(a.astype(BF16))
        return ws, [carries[h] + incls[h][0:1, :] for h in range(heads)]

    def values(j, ws, accs):
        vt = vt_ref[j]
        return [accs[h] + _dot(vt[h * HEAD_DIM:(h + 1) * HEAD_DIM, :], ws[h]) for h in range(heads)]

    carries = [jnp.zeros((1, tile), F32)] * heads
    accs = [jnp.zeros((HEAD_DIM, tile), F32)] * heads

    def load(scr, slot):
        return [scr[slot, h] for h in range(heads)]

    def store(scr, slot, xs):
        for h in range(heads):
            scr[slot, h] = xs[h]

    def stage(j, slot, carries, accs, look_ahead=True):
        accs = values(j + 1, load(a_scr, 1 - slot), accs)
        if look_ahead:
            store(z_scr, 1 - slot, logits(jnp.maximum(j - 1, 0)))
        ws, carries = weights(load(z_scr, slot), carries, None)
        store(a_scr, slot, ws)
        return carries, accs

    z_diag = logits(i)
    store(z_scr, 0, logits(jnp.maximum(i - 1, 0)))
    ws, carries = weights(z_diag, carries, key_before_query)
    store(a_scr, 1, ws)

    def trip(s, state):
        j = i - 1 - 2 * s
        carries, accs = stage(j, 0, *state)
        return stage(j - 1, 1, carries, accs)

    carries, accs = lax.fori_loop(0, i // 2, trip, (carries, accs))

    def odd_tail(carries, accs):
        _, accs = stage(0, 0, carries, accs, look_ahead=False)
        return values(0, load(a_scr, 0), accs)

    accs = lax.cond(i % 2 == 1, odd_tail, lambda carries, accs: values(0, load(a_scr, 1), accs), carries, accs)
    ot_ref[...] = jnp.concatenate(accs, axis=0).astype(BF16)


def _attn_sample_part(g, n_g, q_ref, kn_ref, vn_ref, bias_ref, tri_ref, kp_refs, vp_refs, o_ref,
                      qbd_scr, carry_scr, acc_scr, t_seq):
    pages_per_step = len(kp_refs)
    rows = N_HEADS * t_seq
    d = acc_scr.shape[1]
    page = kp_refs[0].shape[1]
    own_head = (lax.broadcasted_iota(jnp.int32, (rows, d), 0) // t_seq
                == lax.broadcasted_iota(jnp.int32, (rows, d), 1) // HEAD_DIM)
    tri = tri_ref[...]

    def weights(z, carry, mask):
        sp = _softplus(z)
        if mask is not None:
            sp = jnp.where(mask, sp, 0.0)
        sp = sp.astype(BF16)
        ws = []
        for p in range(z.shape[1] // page):
            cols = slice(p * page, (p + 1) * page)
            r = _dot(sp[:, cols], tri)
            a = jnp.exp(z[:, cols] - r[:, :page] - carry)
            if mask is not None:
                a = jnp.where(mask, a, 0.0)
            ws.append(a.astype(BF16))
            carry = carry + r[:, page:]
        return (ws[0] if len(ws) == 1 else jnp.concatenate(ws, axis=1)), carry

    @pl.when(g == 0)
    def _():
        qt = jnp.concatenate([q_ref[...]] * N_HEADS, axis=0)
        qbd = jnp.where(own_head, qt, 0.0).astype(BF16)
        qbd_scr[...] = qbd
        pad = jnp.zeros((page - t_seq, d), F32)
        kc = jnp.concatenate([kn_ref[...], pad], axis=0).astype(BF16)
        vc = jnp.concatenate([vn_ref[...], pad], axis=0).astype(BF16)
        key_idx = lax.broadcasted_iota(jnp.int32, (rows, page), 1)
        qry_idx = lax.broadcasted_iota(jnp.int32, (rows, page), 0) % t_seq
        w, carry = weights(_dot_nt(qbd, kc) + bias_ref[...], jnp.zeros((rows, LANES), F32), key_idx < qry_idx)
        acc_scr[...] = _dot(w, vc)
        carry_scr[...] = carry

    k_all = jnp.concatenate([r[...].astype(BF16) for r in kp_refs], axis=1)
    v_all = jnp.concatenate([r[...].astype(BF16) for r in vp_refs], axis=1)
    bias_all = jnp.concatenate([bias_ref[...]] * pages_per_step, axis=1)
    w, carry = weights(_dot(qbd_scr[...], k_all) + bias_all, carry_scr[...], None)
    acc_scr[...] += _dot_nt(w, v_all)
    carry_scr[...] = carry

    def finish():
        @pl.when(g == n_g - 1)
        def _():
            m = jnp.where(own_head, acc_scr[...], 0.0)
            out = m[0:t_seq, :]
            for h in range(1, N_HEADS):
                out = out + m[h * t_seq:(h + 1) * t_seq, :]
            o_ref[...] = out

    return finish


def _attn_body(pt_ref, bias_ref, qt_ref, k_ref, vt_ref, tri_ref, q_ref, kn_ref, vn_ref, brow_ref, tris_ref, *rest,
               tile, heads, t_seq, pages_per_step, steps_per_seq):
    kp_refs = rest[:pages_per_step]
    vp_refs = rest[pages_per_step:2 * pages_per_step]
    ot_ref, o_ref, z_scr, a_scr, qbd_scr, carry_scr, acc_scr = rest[2 * pages_per_step:]
    step = (pl.program_id(0) * pl.num_programs(1) + pl.program_id(1)) * pl.num_programs(2) + pl.program_id(2)
    finish_sample = _attn_sample_part(step % steps_per_seq, steps_per_seq, q_ref, kn_ref, vn_ref, brow_ref,
                                      tris_ref, kp_refs, vp_refs, o_ref, qbd_scr, carry_scr, acc_scr, t_seq)
    _attn_prompt_part(pl.program_id(1), pl.program_id(2), bias_ref, qt_ref, k_ref, vt_ref, tri_ref, ot_ref,
                      z_scr, a_scr, tile, heads)
    finish_sample()


def _attn(bias, qt, k, vtb, page_table, q, k_new, v_new, bias_rows, cache_kt, cache_vt, layer, t_seq, tile, heads):
    b, d, t = qt.shape
    rows_p = heads * HEAD_DIM
    n_tiles = t // tile
    n_groups = d // rows_p
    n, _ = q.shape
    n_seq, n_pages = page_table.shape
    page = cache_kt.shape[3]
    rows_s = N_HEADS * t_seq
    steps = b * n_groups * n_tiles
    assert (n_seq * n_pages) % steps == 0 and steps % n_seq == 0
    pages_per_step = n_seq * n_pages // steps
    steps_per_seq = steps // n_seq
    tri = (jnp.arange(tile)[None, :] >= jnp.arange(tile)[:, None]).astype(BF16)
    tris = jnp.concatenate([(jnp.arange(page)[:, None] >= jnp.arange(page)[None, :]).astype(BF16),
                            jnp.ones((page, LANES), BF16)], axis=1)

    def step_of(bi, g, i):
        return (bi * n_groups + g) * n_tiles + i

    seq_rows = pl.BlockSpec((t_seq, d), lambda bi, g, i, pt: (step_of(bi, g, i) // steps_per_seq, 0))

    def page_spec(p):
        def index(bi, g, i, pt):
            s = step_of(bi, g, i)
            nth = (s % steps_per_seq) * pages_per_step + p
            return (layer, pt[(s // steps_per_seq) * n_pages + (n_pages - 1 - nth)], 0, 0)
        return pl.BlockSpec((None, None, d, page), index)

    const = lambda a: pl.BlockSpec(a.shape, lambda bi, g, i, pt: (0,) * a.ndim)
    grid_spec = pltpu.PrefetchScalarGridSpec(
        num_scalar_prefetch=1,
        grid=(b, n_groups, n_tiles),
        in_specs=[pl.BlockSpec(memory_space=pltpu.SMEM),
                  pl.BlockSpec((None, rows_p, tile), lambda bi, g, i, pt: (bi, g, i)),
                  pl.BlockSpec((None, t, rows_p), lambda bi, g, i, pt: (bi, 0, g)),
                  pl.BlockSpec((None, n_tiles, rows_p, tile), lambda bi, g, i, pt: (bi, 0, g, 0)),
                  const(tri),
                  seq_rows, seq_rows, seq_rows, const(bias_rows), const(tris)]
                 + [page_spec(p) for p in range(pages_per_step)] * 2,
        out_specs=[pl.BlockSpec((None, rows_p, tile), lambda bi, g, i, pt: (bi, g, i)), seq_rows],
        scratch_shapes=[pltpu.VMEM((2, heads, tile, tile), F32), pltpu.VMEM((2, heads, tile, tile), BF16),
                        pltpu.VMEM((rows_s, d), BF16), pltpu.VMEM((rows_s, LANES), F32),
                        pltpu.VMEM((rows_s, d), F32)],
    )
    return pl.pallas_call(
        functools.partial(_attn_body, tile=tile, heads=heads, t_seq=t_seq, pages_per_step=pages_per_step,
                          steps_per_seq=steps_per_seq),
        grid_spec=grid_spec,
        out_shape=[jax.ShapeDtypeStruct((b, d, t), BF16), jax.ShapeDtypeStruct((n, d), F32)],
        compiler_params=_params(3),
        name="attn",
    )(page_table.reshape(-1), bias, qt, k, vtb, tri, q, k_new, v_new, bias_rows, tris,
      *([cache_kt] * pages_per_step), *([cache_vt] * pages_per_step))


def _merge_body(x_ref, sa_ref, o_ref, cp_ref, pre_ref, post_ref, win_ref, wa_ref, wb_ref, wc_ref,
                wo_ref, y_ref, *, o_feature_major):
    x = x_ref[...]
    d = x.shape[-1]
    o_g = _proj_columns(sa_ref.shape[-1], cp_ref.shape[-1])[-1]
    wgate_ref = win_ref.at[:, o_g:o_g + 3 * d]
    u = _rms(x, pre_ref[...]).astype(BF16)
    if o_feature_major:
        branch_b = _dot_tn(o_ref[...], wb_ref[...])
    else:
        branch_b = _dot(o_ref[...].astype(BF16), wb_ref[...])
    merged = jax.nn.sigmoid(_dot(u, wgate_ref[:, 0:d])) * _dot(sa_ref[...], wa_ref[...])
    merged = merged + jax.nn.sigmoid(_dot(u, wgate_ref[:, d:2 * d])) * branch_b
    merged = merged + jax.nn.sigmoid(_dot(u, wgate_ref[:, 2 * d:3 * d])) * _dot(cp_ref[...], wc_ref[...])
    m = _dot(merged.astype(BF16), wo_ref[...])
    y_ref[...] = x + _rms(m, post_ref[...])


def _merge(x, sa, o, cp, pre_g, post_g, w_in, w_a, w_b, w_c, w_o, layer, tile, o_feature_major):
    b, t, d = x.shape
    rows = lambda w: pl.BlockSpec((None, tile, w), lambda bi, i: (bi, i, 0))
    o_spec = (pl.BlockSpec((None, o.shape[1], tile), lambda bi, i: (bi, 0, i)) if o_feature_major
              else rows(o.shape[2]))
    full = lambda a: _resident(a.shape)
    return pl.pallas_call(
        functools.partial(_merge_body, o_feature_major=o_feature_major),
        grid=(b, t // tile),
        in_specs=[rows(d), rows(sa.shape[2]), o_spec, rows(cp.shape[2]),
                  full(pre_g), full(post_g)] + [_layer_of(w, layer) for w in (w_in, w_a, w_b, w_c, w_o)],
        out_specs=rows(d),
        out_shape=jax.ShapeDtypeStruct((b, t, d), F32),
        compiler_params=_params(2),
        name="merge",
    )(x, sa, o, cp, pre_g, post_g, w_in, w_a, w_b, w_c, w_o)


ROW_TILE = 512
MIX_TILE = 256
ATTN_TILE = 256
ATTN_HEADS = 4


def _pad_rows_front(a, multiple):
    r = a.shape[-2]
    pad = -r % multiple
    return jnp.pad(a, [(0, 0)] * (a.ndim - 2) + [(pad, 0), (0, 0)])


def kernel(x_prompt, x_sample, cache_k, cache_v, state_conv_a, state_conv_c, page_table, ffn1_pre_g, ffn1_post_g, ffn1_w_gate, ffn1_w_up, ffn1_w_down, mix_pre_g, mix_post_g, w_in, conv_a_w, conv_a_b, ln_a_g, ln_a_b, w_a_out, w_b_out, sb_bias, conv_c_w, w_c_out, w_o, ffn2_pre_g, ffn2_post_g, ffn2_w_gate, ffn2_w_up, ffn2_w_down):
    b, t, d = x_prompt.shape
    n_seq, t_seq, _ = x_sample.shape
    depth, n_pool, page = cache_k.shape[:3]
    wa, d_a = conv_a_w.shape[1:]
    wc, d_c = conv_c_w.shape[1:]
    n = n_seq * t_seq
    o_q, o_k, o_v, o_c, o_g = _proj_columns(d_a, d_c)

    xp = x_prompt.reshape(b * t, d)
    xs = x_sample.reshape(n, d)
    ckt = jnp.transpose(cache_k, (0, 1, 3, 4, 2)).reshape(depth, n_pool, D_ATTN, page)
    cvt = jnp.transpose(cache_v, (0, 1, 3, 4, 2)).reshape(depth, n_pool, D_ATTN, page)
    zero_a = jnp.zeros((b, -(-(wa - 1) // SUBLANES) * SUBLANES, d_a), F32)
    zero_c = jnp.zeros((b, -(-(wc - 1) // SUBLANES) * SUBLANES, d_c), F32)
    sta_pad = _pad_rows_front(state_conv_a, SUBLANES)
    stc_pad = _pad_rows_front(state_conv_c, SUBLANES)
    vec = lambda a: a.reshape(1, -1)
    f1_w = [w.astype(BF16) for w in (ffn1_w_gate, ffn1_w_up, ffn1_w_down)]
    f2_w = [w.astype(BF16) for w in (ffn2_w_gate, ffn2_w_up, ffn2_w_down)]
    w_in_b = w_in.astype(BF16)
    out_w = [w.astype(BF16) for w in (w_a_out, w_b_out, w_c_out, w_o)]

    kv_stacks = None
    outs = [[] for _ in range(6)]
    for l in range(depth):
        f1 = (vec(ffn1_pre_g[l]), vec(ffn1_post_g[l]), *f1_w, l)
        f2 = (vec(ffn2_pre_g[l]), vec(ffn2_post_g[l]), *f2_w, l)
        proj_w = (vec(mix_pre_g[l]), w_in_b, w_in_b[l, :, o_q:o_k].T, w_in_b[l, :, o_v:o_c].T)
        conv = (conv_a_w[l], vec(conv_a_b[l]), vec(ln_a_g[l]), vec(ln_a_b[l]), conv_c_w[l])
        merge_w = (vec(mix_pre_g[l]), vec(mix_post_g[l]), w_in_b, *out_w, l)
        bias_rows = jnp.broadcast_to(jnp.repeat(sb_bias[l], t_seq)[:, None], (N_HEADS * t_seq, LANES))

        xp = _ffn(xp, *f1, tile=ROW_TILE)
        xs = _ffn(xs, *f1, tile=n)
        qt, kb, kt_all, vt_all, vtb, sa_p, cp_p, na_p, nc_p = _mixproj_prompt(
            xp.reshape(b, t, d), *proj_w, zero_a, zero_c, *conv, kv_stacks, l, depth, tile=MIX_TILE)
        kv_stacks = (kt_all, vt_all)
        q, k, v, sa_s, cp_s, na_s, nc_s = _mixproj_sample(xs, *proj_w, sta_pad[l], stc_pad[l], *conv, l,
                                                          n_seq=n_seq, t_seq=t_seq)
        ot, o = _attn(sb_bias[l], qt, kb, vtb, page_table, q, k, v, bias_rows, ckt, cvt, l, t_seq,
                      tile=ATTN_TILE, heads=ATTN_HEADS)
        xp = _merge(xp.reshape(b, t, d), sa_p, ot, cp_p, *merge_w, tile=ROW_TILE, o_feature_major=True)
        xs = _merge(xs[None], sa_s[None], o[None], cp_s[None], *merge_w, tile=n, o_feature_major=False)[0]
        xp = _ffn(xp.reshape(b * t, d), *f2, tile=ROW_TILE)
        xs = _ffn(xs, *f2, tile=n)
        for lst, val in zip(outs, (na_p, nc_p, k.reshape(n_seq, t_seq, N_HEADS, HEAD_DIM),
                                   v.reshape(n_seq, t_seq, N_HEADS, HEAD_DIM), na_s, nc_s)):
            lst.append(val)

    kv_out = lambda a: jnp.transpose(a.reshape(depth, b, N_HEADS, HEAD_DIM, t), (0, 1, 4, 2, 3))
    return (xp.reshape(b, t, d), xs.reshape(n_seq, t_seq, d), kv_out(kv_stacks[0]), kv_out(kv_stacks[1]),
            *[jnp.stack(o) for o in outs])
```

```python
import functools

import jax
import jax.numpy as jnp
from jax import lax
from jax.experimental import pallas as pl
from jax.experimental.pallas import tpu as pltpu

EPS = 1e-6
N_HEADS = 16
HEAD_DIM = 64
D_ATTN = N_HEADS * HEAD_DIM
LANES = 128
SUBLANES = 8
VMEM_LIMIT = 56 * 1024 * 1024
SOFTPLUS_LINEAR_ABOVE = 60.0

F32 = jnp.float32
BF16 = jnp.bfloat16


def _params(n_grid):
    return pltpu.CompilerParams(dimension_semantics=("arbitrary",) * n_grid,
                                vmem_limit_bytes=VMEM_LIMIT)


def _resident(shape):
    zeros = (0,) * len(shape)
    return pl.BlockSpec(shape, lambda *_: zeros, pipeline_mode=pl.Buffered(1))


def _layer_of(stacked, layer):
    index = (layer,) + (0,) * (stacked.ndim - 1)
    return pl.BlockSpec((None,) + stacked.shape[1:], lambda *_: index, pipeline_mode=pl.Buffered(1))


def _rms(x, g):
    return x * lax.rsqrt(jnp.mean(x * x, axis=-1, keepdims=True) + EPS) * g


def _silu(x):
    return x * jax.nn.sigmoid(x)


def _softplus(z):
    return jnp.where(z > SOFTPLUS_LINEAR_ABOVE, z, jnp.log(1.0 + jnp.exp(z)))


def _dot(a, b):
    return jnp.dot(a, b, preferred_element_type=F32)


def _dot_nt(a, b):
    return lax.dot_general(a, b, (((1,), (1,)), ((), ())), preferred_element_type=F32)


def _dot_tn(a, b):
    return lax.dot_general(a, b, (((0,), (0,)), ((), ())), preferred_element_type=F32)


def _ffn_body(x_ref, pre_ref, post_ref, wg_ref, wu_ref, wd_ref, o_ref):
    x = x_ref[...]
    xn = _rms(x, pre_ref[...]).astype(BF16)
    h = (_silu(_dot(xn, wg_ref[...])) * _dot(xn, wu_ref[...])).astype(BF16)
    y = _dot(h, wd_ref[...])
    o_ref[...] = x + 0.5 * _rms(y, post_ref[...])


def _ffn(x, pre_g, post_g, wg, wu, wd, layer, tile):
    n, d = x.shape
    row = pl.BlockSpec((tile, d), lambda i: (i, 0))
    return pl.pallas_call(
        _ffn_body,
        grid=(n // tile,),
        in_specs=[row, _resident((1, d)), _resident((1, d)),
                  _layer_of(wg, layer), _layer_of(wu, layer), _layer_of(wd, layer)],
        out_specs=row,
        out_shape=jax.ShapeDtypeStruct((n, d), F32),
        compiler_params=_params(1),
        name="ffn",
    )(x, pre_g, post_g, wg, wu, wd)


def _proj_columns(d_a, d_c):
    o_q = 2 * d_a
    o_k = o_q + D_ATTN
    o_v = o_k + D_ATTN
    o_c = o_v + D_ATTN
    o_g = o_c + 3 * d_c
    return o_q, o_k, o_v, o_c, o_g


def _layernorm_silu(t, g, b):
    mu = jnp.mean(t, axis=-1, keepdims=True)
    c = t - mu
    var = jnp.mean(c * c, axis=-1, keepdims=True)
    return _silu(c * lax.rsqrt(var + EPS) * g + b)


def _conv_taps(w_ref, b, scr, phase_scr, first, rows, n_taps, lead=()):
    acc = b
    for phase in range(SUBLANES):
        taps = [j for j in range(n_taps) if (first + j) % SUBLANES == phase]
        if not taps:
            continue
        lo = first + taps[0]
        span = taps[-1] - taps[0] + rows
        if phase == 0:
            src, base = scr, lo
        else:
            phase_scr[lead + (slice(0, span), slice(None))] = scr[lead + (slice(lo, lo + span), slice(None))]
            src, base = phase_scr, 0
        for j in taps:
            off = base + j - taps[0]
            term = w_ref[j:j + 1, :] * src[lead + (slice(off, off + rows), slice(None))]
            acc = term if acc is None else acc + term
    return acc


def _mixproj_prompt_body(x_ref, pre_ref, w_ref, wqt_ref, wvt_ref, sta_ref, stc_ref,
                         caw_ref, cab_ref, lng_ref, lnb_ref, ccw_ref, *rest, tile, wa, wc, pad_a, pad_c):
    qt_ref, kb_ref, kt_ref, vt_ref, vtb_ref, sa_ref, cp_ref, na_ref, nc_ref, xa_scr, xc_scr, ph_scr = rest[-12:]
    i = pl.program_id(1)
    halo_a = wa - 1
    halo_c = wc - 1
    d_a = xa_scr.shape[1]
    d_c = xc_scr.shape[1]
    o_q, o_k, o_v, o_c, o_g = _proj_columns(d_a, d_c)

    @pl.when(i == 0)
    def _():
        xa_scr[0:pad_a, :] = sta_ref[...]
        xc_scr[0:pad_c, :] = stc_ref[...]

    @pl.when(i > 0)
    def _():
        xa_scr[pad_a - halo_a:pad_a, :] = xa_scr[tile + pad_a - halo_a:tile + pad_a, :]
        xc_scr[pad_c - halo_c:pad_c, :] = xc_scr[tile + pad_c - halo_c:tile + pad_c, :]

    u = _rms(x_ref[...], pre_ref[...]).astype(BF16)
    qt_ref[...] = (_dot_nt(wqt_ref[...], u) * (HEAD_DIM ** -0.5)).astype(BF16)
    k = _dot(u, w_ref[:, o_k:o_v])
    kb_ref[...] = k.astype(BF16)
    kt_ref[...] = k.T
    vt = _dot_nt(wvt_ref[...], u)
    vt_ref[...] = vt
    vtb_ref[...] = vt.astype(BF16)

    a2 = _dot(u, w_ref[:, 0:o_q])
    xa_scr[pad_a:pad_a + tile, :] = a2[:, :d_a] * jax.nn.sigmoid(a2[:, d_a:])
    conv = _conv_taps(caw_ref, cab_ref[...], xa_scr, ph_scr, pad_a - halo_a, tile, wa)
    sa_ref[...] = _layernorm_silu(conv, lng_ref[...], lnb_ref[...]).astype(BF16)

    c3 = _dot(u, w_ref[:, o_c:o_g])
    xc_scr[pad_c:pad_c + tile, :] = c3[:, 2 * d_c:] * c3[:, :d_c]
    cconv = _conv_taps(ccw_ref, None, xc_scr, ph_scr, pad_c - halo_c, tile, wc)
    cp_ref[...] = (c3[:, d_c:2 * d_c] * cconv).astype(BF16)

    @pl.when(i == pl.num_programs(1) - 1)
    def _():
        na_ref[...] = xa_scr[tile + pad_a - halo_a:tile + pad_a, :]
        nc_ref[...] = xc_scr[tile + pad_c - halo_c:tile + pad_c, :]


def _mixproj_prompt(x, pre_g, w_in, w_qt, w_vt, sta_pad, stc_pad, caw, cab, lng, lnb, ccw, kv_stacks, layer,
                    depth, tile):
    b, t, d = x.shape
    wa, d_a = caw.shape
    wc, d_c = ccw.shape
    pad_a, pad_c = sta_pad.shape[1], stc_pad.shape[1]
    rows = lambda w: pl.BlockSpec((None, tile, w), lambda bi, i: (bi, i, 0))
    cols = pl.BlockSpec((None, D_ATTN, tile), lambda bi, i: (bi, 0, i))
    layer_cols = pl.BlockSpec((None, None, D_ATTN, tile), lambda bi, i: (layer, bi, 0, i))
    per_seq = lambda r, w: pl.BlockSpec((None, r, w), lambda bi, i: (bi, 0, 0))
    full = lambda a: _resident(a.shape)
    body = functools.partial(_mixproj_prompt_body, tile=tile, wa=wa, wc=wc, pad_a=pad_a, pad_c=pad_c)
    args = [x, pre_g, w_in, w_qt, w_vt, sta_pad, stc_pad, caw, cab, lng, lnb, ccw]
    in_specs = [rows(d), full(pre_g), _layer_of(w_in, layer), full(w_qt), full(w_vt),
                per_seq(pad_a, d_a), per_seq(pad_c, d_c),
                full(caw), full(cab), full(lng), full(lnb), full(ccw)]
    aliases = {}
    if kv_stacks is not None:
        aliases = {len(args): 2, len(args) + 1: 3}
        args += list(kv_stacks)
        in_specs += [pl.BlockSpec(memory_space=pl.ANY)] * 2
    return pl.pallas_call(
        body,
        grid=(b, t // tile),
        in_specs=in_specs,
        out_specs=[cols, rows(D_ATTN), layer_cols, layer_cols,
                   pl.BlockSpec((None, None, D_ATTN, tile), lambda bi, i: (bi, i, 0, 0)),
                   rows(d_a), rows(d_c), per_seq(wa - 1, d_a), per_seq(wc - 1, d_c)],
        out_shape=[jax.ShapeDtypeStruct((b, D_ATTN, t), BF16),
                   jax.ShapeDtypeStruct((b, t, D_ATTN), BF16),
                   jax.ShapeDtypeStruct((depth, b, D_ATTN, t), F32),
                   jax.ShapeDtypeStruct((depth, b, D_ATTN, t), F32),
                   jax.ShapeDtypeStruct((b, t // tile, D_ATTN, tile), BF16),
                   jax.ShapeDtypeStruct((b, t, d_a), BF16),
                   jax.ShapeDtypeStruct((b, t, d_c), BF16),
                   jax.ShapeDtypeStruct((b, wa - 1, d_a), F32),
                   jax.ShapeDtypeStruct((b, wc - 1, d_c), F32)],
        scratch_shapes=[pltpu.VMEM((pad_a + tile, d_a), F32), pltpu.VMEM((pad_c + tile, d_c), F32),
                        pltpu.VMEM((pad_a + tile, max(d_a, d_c)), F32)],
        input_output_aliases=aliases,
        compiler_params=_params(2),
        name="mixproj_prompt",
    )(*args)


def _mixproj_sample_body(x_ref, pre_ref, w_ref, wqt_ref, wvt_ref, sta_ref, stc_ref,
                         caw_ref, cab_ref, lng_ref, lnb_ref, ccw_ref,
                         q_ref, k_ref, v_ref, sa_ref, cp_ref, na_ref, nc_ref,
                         xa_scr, xc_scr, ph_scr, *, n_seq, t_seq, wa, wc, pad_a, pad_c):
    halo_a = wa - 1
    halo_c = wc - 1
    d_a = xa_scr.shape[2]
    d_c = xc_scr.shape[2]
    n = n_seq * t_seq
    o_q, o_k, o_v, o_c, o_g = _proj_columns(d_a, d_c)
    u = _rms(x_ref[...], pre_ref[...]).astype(BF16)
    q_ref[...] = _dot_nt(u, wqt_ref[...]) * (HEAD_DIM ** -0.5)
    k_ref[...] = _dot(u, w_ref[:, o_k:o_v])
    v_ref[...] = _dot_nt(u, wvt_ref[...])

    a2 = _dot(u, w_ref[:, 0:o_q])
    xa_scr[:, 0:pad_a, :] = sta_ref[...]
    xa_scr[:, pad_a:pad_a + t_seq, :] = (a2[:, :d_a] * jax.nn.sigmoid(a2[:, d_a:])).reshape(n_seq, t_seq, d_a)
    conv = _conv_taps(caw_ref, cab_ref[...], xa_scr, ph_scr, pad_a - halo_a, t_seq, wa, lead=(slice(None),))
    sa_ref[...] = _layernorm_silu(conv, lng_ref[...], lnb_ref[...]).reshape(n, d_a).astype(BF16)

    c3 = _dot(u, w_ref[:, o_c:o_g])
    xc_scr[:, 0:pad_c, :] = stc_ref[...]
    xc_scr[:, pad_c:pad_c + t_seq, :] = (c3[:, 2 * d_c:] * c3[:, :d_c]).reshape(n_seq, t_seq, d_c)
    cconv = _conv_taps(ccw_ref, None, xc_scr, ph_scr, pad_c - halo_c, t_seq, wc, lead=(slice(None),))
    cp_ref[...] = (c3[:, d_c:2 * d_c] * cconv.reshape(n, d_c)).astype(BF16)

    na_ref[...] = xa_scr[:, pad_a + t_seq - halo_a:pad_a + t_seq, :]
    nc_ref[...] = xc_scr[:, pad_c + t_seq - halo_c:pad_c + t_seq, :]


def _mixproj_sample(x, pre_g, w_in, w_qt, w_vt, sta_pad, stc_pad, caw, cab, lng, lnb, ccw, layer,
                    n_seq, t_seq):
    n, d = x.shape
    wa, d_a = caw.shape
    wc, d_c = ccw.shape
    pad_a, pad_c = sta_pad.shape[1], stc_pad.shape[1]
    body = functools.partial(_mixproj_sample_body, n_seq=n_seq, t_seq=t_seq, wa=wa, wc=wc,
                             pad_a=pad_a, pad_c=pad_c)
    full = lambda a: _resident(a.shape)
    out_shape = [jax.ShapeDtypeStruct((n, D_ATTN), F32),
                 jax.ShapeDtypeStruct((n, D_ATTN), F32),
                 jax.ShapeDtypeStruct((n, D_ATTN), F32),
                 jax.ShapeDtypeStruct((n, d_a), BF16),
                 jax.ShapeDtypeStruct((n, d_c), BF16),
                 jax.ShapeDtypeStruct((n_seq, wa - 1, d_a), F32),
                 jax.ShapeDtypeStruct((n_seq, wc - 1, d_c), F32)]
    args = (x, pre_g, w_in, w_qt, w_vt, sta_pad, stc_pad, caw, cab, lng, lnb, ccw)
    return pl.pallas_call(
        body,
        grid=(1,),
        in_specs=[_layer_of(a, layer) if a is w_in else full(a) for a in args],
        out_specs=[pl.BlockSpec(s.shape, lambda i, nd=len(s.shape): (0,) * nd) for s in out_shape],
        out_shape=out_shape,
        scratch_shapes=[pltpu.VMEM((n_seq, pad_a + t_seq, d_a), F32),
                        pltpu.VMEM((n_seq, pad_c + t_seq, d_c), F32),
                        pltpu.VMEM((n_seq, pad_a + t_seq, max(d_a, d_c)), F32)],
        compiler_params=_params(1),
        name="mixproj_sample",
    )(*args)


def _attn_prompt_part(g, i, bias_ref, qt_ref, k_ref, vt_ref, tri_ref, ot_ref, z_scr, a_scr, tile, heads):
    tri = tri_ref[...]
    qt = qt_ref[...]
    zero = jnp.zeros((HEAD_DIM, tile), BF16)
    pair = LANES // HEAD_DIM
    qm = []
    for h in range(heads):
        parts = [zero] * pair
        parts[h % pair] = qt[h * HEAD_DIM:(h + 1) * HEAD_DIM, :]
        qm.append(jnp.concatenate(parts, axis=0))
    ones = jnp.ones((tile, LANES), BF16)
    row = lax.broadcasted_iota(jnp.int32, (LANES, tile), 0)
    qe = []
    for h in range(heads):
        b = jnp.full((LANES, tile), bias_ref[g * heads + h], F32)
        hi = b.astype(BF16).astype(F32)
        mid = (b - hi).astype(BF16).astype(F32)
        lo = b - hi - mid
        piece = jnp.where(row == 0, hi, jnp.where(row == 1, mid, jnp.where(row == 2, lo, 0.0)))
        qe.append(jnp.concatenate([qm[h], piece.astype(BF16)], axis=0))
    key_before_query = (lax.broadcasted_iota(jnp.int32, (tile, tile), 0)
                        < lax.broadcasted_iota(jnp.int32, (tile, tile), 1))

    def logits(j):
        kt = k_ref[pl.ds(pl.multiple_of(j * tile, tile), tile), :]
        return [_dot(jnp.concatenate([kt[:, (h // pair) * LANES:(h // pair + 1) * LANES], ones], axis=1), qe[h])
                for h in range(heads)]

    def weights(zs, carries, mask):
        sps = []
        for h in range(heads):
            sp = _softplus(zs[h])
            if mask is not None:
                sp = jnp.where(mask, sp, 0.0)
            sps.append(sp.astype(BF16))
        incls = [_dot(tri, sps[h]) for h in range(heads)]
        ws = []
        for h in range(heads):
            a = jnp.exp(zs[h] - incls[h] - carries[h])
            if mask is not None:
                a = jnp.where(mask, a, 0.0)
            ws.append(a.astype(BF16))
        return ws, [carries[h] + incls[h][0:1, :] for h in range(heads)]

    def values(j, ws, accs):
        vt = vt_ref[j]
        return [accs[h] + _dot(vt[h * HEAD_DIM:(h + 1) * HEAD_DIM, :], ws[h]) for h in range(heads)]

    carries = [jnp.zeros((1, tile), F32)] * heads
    accs = [jnp.zeros((HEAD_DIM, tile), F32)] * heads

    def load(scr, slot):
        return [scr[slot, h] for h in range(heads)]

    def store(scr, slot, xs):
        for h in range(heads):
            scr[slot, h] = xs[h]

    def stage(j, slot, carries, accs, look_ahead=True):
        accs = values(j + 1, load(a_scr, 1 - slot), accs)
        if look_ahead:
            store(z_scr, 1 - slot, logits(jnp.maximum(j - 1, 0)))
        ws, carries = weights(load(z_scr, slot), carries, None)
        store(a_scr, slot, ws)
        return carries, accs

    z_diag = logits(i)
    store(z_scr, 0, logits(jnp.maximum(i - 1, 0)))
    ws, carries = weights(z_diag, carries, key_before_query)
    store(a_scr, 1, ws)

    def trip(s, state):
        j = i - 1 - 2 * s
        carries, accs = stage(j, 0, *state)
        return stage(j - 1, 1, carries, accs)

    carries, accs = lax.fori_loop(0, i // 2, trip, (carries, accs))

    def odd_tail(carries, accs):
        _, accs = stage(0, 0, carries, accs, look_ahead=False)
        return values(0, load(a_scr, 0), accs)

    accs = lax.cond(i % 2 == 1, odd_tail, lambda carries, accs: values(0, load(a_scr, 1), accs), carries, accs)
    ot_ref[...] = jnp.concatenate(accs, axis=0).astype(BF16)


MXU_DEPTH = 256


def _attn_sample_part(g, n_g, q_ref, kn_ref, vn_ref, bias_ref, tri_ref, kp_refs, vp_refs, o_ref,
                      qbd_scr, carry_scr, acc_scr, t_seq):
    pages_per_step = len(kp_refs)
    rows = N_HEADS * t_seq
    d = N_HEADS * HEAD_DIM
    page = kp_refs[0].shape[1]
    grp_heads = MXU_DEPTH // HEAD_DIM
    n_grp = N_HEADS // grp_heads
    grp_rows = grp_heads * t_seq
    own_head = ((lax.broadcasted_iota(jnp.int32, (rows, MXU_DEPTH), 0) // t_seq) % grp_heads
                == lax.broadcasted_iota(jnp.int32, (rows, MXU_DEPTH), 1) // HEAD_DIM)
    tri = tri_ref[...]
    rows_of = lambda x, i: x[i * grp_rows:(i + 1) * grp_rows]
    feats_of = lambda x, i: x[i * MXU_DEPTH:(i + 1) * MXU_DEPTH]

    def weights(z, carry, mask):
        sp = _softplus(z)
        if mask is not None:
            sp = jnp.where(mask, sp, 0.0)
        sp = sp.astype(BF16)
        ws = []
        for p in range(z.shape[1] // page):
            cols = slice(p * page, (p + 1) * page)
            r = _dot(sp[:, cols], tri)
            a = jnp.exp(z[:, cols] - r[:, :page] - carry)
            if mask is not None:
                a = jnp.where(mask, a, 0.0)
            ws.append(a.astype(BF16))
            carry = carry + r[:, page:]
        return (ws[0] if len(ws) == 1 else jnp.concatenate(ws, axis=1)), carry

    @pl.when(g == 0)
    def _():
        q = q_ref[...]
        qbd = jnp.concatenate([jnp.concatenate([q[:, i * MXU_DEPTH:(i + 1) * MXU_DEPTH]] * grp_heads, axis=0)
                               for i in range(n_grp)], axis=0)
        qbd = jnp.where(own_head, qbd, 0.0).astype(BF16)
        qbd_scr[...] = qbd
        pad = jnp.zeros((page - t_seq, d), F32)
        kc = jnp.concatenate([kn_ref[...], pad], axis=0).astype(BF16)
        vc = jnp.concatenate([vn_ref[...], pad], axis=0).astype(BF16)
        key_idx = lax.broadcasted_iota(jnp.int32, (rows, page), 1)
        qry_idx = lax.broadcasted_iota(jnp.int32, (rows, page), 0) % t_seq
        z = jnp.concatenate([_dot_nt(rows_of(qbd, i), kc[:, i * MXU_DEPTH:(i + 1) * MXU_DEPTH])
                             for i in range(n_grp)], axis=0)
        w, carry = weights(z + bias_ref[...], jnp.zeros((rows, LANES), F32), key_idx < qry_idx)
        acc_scr[...] = jnp.concatenate([_dot(rows_of(w, i), vc[:, i * MXU_DEPTH:(i + 1) * MXU_DEPTH])
                                        for i in range(n_grp)], axis=0)
        carry_scr[...] = carry

    k_all = jnp.concatenate([r[...].astype(BF16) for r in kp_refs], axis=1)
    v_all = jnp.concatenate([r[...].astype(BF16) for r in vp_refs], axis=1)
    bias_all = jnp.concatenate([bias_ref[...]] * pages_per_step, axis=1)
    qbd = qbd_scr[...]
    z = jnp.concatenate([_dot(rows_of(qbd, i), feats_of(k_all, i)) for i in range(n_grp)], axis=0)
    w, carry = weights(z + bias_all, carry_scr[...], None)
    acc_scr[...] += jnp.concatenate([_dot_nt(rows_of(w, i), feats_of(v_all, i)) for i in range(n_grp)], axis=0)
    carry_scr[...] = carry

    def finish():
        @pl.when(g == n_g - 1)
        def _():
            m = jnp.where(own_head, acc_scr[...], 0.0)
            outs = []
            for i in range(n_grp):
                mg = rows_of(m, i)
                out = mg[0:t_seq, :]
                for h in range(1, grp_heads):
                    out = out + mg[h * t_seq:(h + 1) * t_seq, :]
                outs.append(out)
            o_ref[...] = jnp.concatenate(outs, axis=1)

    return finish


def _attn_body(pt_ref, bias_ref, qt_ref, k_ref, vt_ref, tri_ref, q_ref, kn_ref, vn_ref, brow_ref, tris_ref, *rest,
               tile, heads, t_seq, pages_per_step, steps_per_seq):
    kp_refs = rest[:pages_per_step]
    vp_refs = rest[pages_per_step:2 * pages_per_step]
    ot_ref, o_ref, z_scr, a_scr, qbd_scr, carry_scr, acc_scr = rest[2 * pages_per_step:]
    step = (pl.program_id(0) * pl.num_programs(1) + pl.program_id(1)) * pl.num_programs(2) + pl.program_id(2)
    finish_sample = _attn_sample_part(step % steps_per_seq, steps_per_seq, q_ref, kn_ref, vn_ref, brow_ref,
                                      tris_ref, kp_refs, vp_refs, o_ref, qbd_scr, carry_scr, acc_scr, t_seq)
    _attn_prompt_part(pl.program_id(1), pl.program_id(2), bias_ref, qt_ref, k_ref, vt_ref, tri_ref, ot_ref,
                      z_scr, a_scr, tile, heads)
    finish_sample()


def _attn(bias, qt, k, vtb, page_table, q, k_new, v_new, bias_rows, cache_kt, cache_vt, layer, t_seq, tile, heads):
    b, d, t = qt.shape
    rows_p = heads * HEAD_DIM
    n_tiles = t // tile
    n_groups = d // rows_p
    n, _ = q.shape
    n_seq, n_pages = page_table.shape
    page = cache_kt.shape[3]
    rows_s = N_HEADS * t_seq
    steps = b * n_groups * n_tiles
    assert (n_seq * n_pages) % steps == 0 and steps % n_seq == 0
    pages_per_step = n_seq * n_pages // steps
    steps_per_seq = steps // n_seq
    tri = (jnp.arange(tile)[None, :] >= jnp.arange(tile)[:, None]).astype(BF16)
    tris = jnp.concatenate([(jnp.arange(page)[:, None] >= jnp.arange(page)[None, :]).astype(BF16),
                            jnp.ones((page, LANES), BF16)], axis=1)

    def step_of(bi, g, i):
        return (bi * n_groups + g) * n_tiles + i

    seq_rows = pl.BlockSpec((t_seq, d), lambda bi, g, i, pt: (step_of(bi, g, i) // steps_per_seq, 0))

    def page_spec(p):
        def index(bi, g, i, pt):
            s = step_of(bi, g, i)
            nth = (s % steps_per_seq) * pages_per_step + p
            return (layer, pt[(s // steps_per_seq) * n_pages + (n_pages - 1 - nth)], 0, 0)
        return pl.BlockSpec((None, None, d, page), index)

    const = lambda a: pl.BlockSpec(a.shape, lambda bi, g, i, pt: (0,) * a.ndim)
    grid_spec = pltpu.PrefetchScalarGridSpec(
        num_scalar_prefetch=1,
        grid=(b, n_groups, n_tiles),
        in_specs=[pl.BlockSpec(memory_space=pltpu.SMEM),
                  pl.BlockSpec((None, rows_p, tile), lambda bi, g, i, pt: (bi, g, i)),
                  pl.BlockSpec((None, t, rows_p), lambda bi, g, i, pt: (bi, 0, g)),
                  pl.BlockSpec((None, n_tiles, rows_p, tile), lambda bi, g, i, pt: (bi, 0, g, 0)),
                  const(tri),
                  seq_rows, seq_rows, seq_rows, const(bias_rows), const(tris)]
                 + [page_spec(p) for p in range(pages_per_step)] * 2,
        out_specs=[pl.BlockSpec((None, rows_p, tile), lambda bi, g, i, pt: (bi, g, i)), seq_rows],
        scratch_shapes=[pltpu.VMEM((2, heads, tile, tile), F32), pltpu.VMEM((2, heads, tile, tile), BF16),
                        pltpu.VMEM((rows_s, MXU_DEPTH), BF16), pltpu.VMEM((rows_s, LANES), F32),
                        pltpu.VMEM((rows_s, MXU_DEPTH), F32)],
    )
    return pl.pallas_call(
        functools.partial(_attn_body, tile=tile, heads=heads, t_seq=t_seq, pages_per_step=pages_per_step,
                          steps_per_seq=steps_per_seq),
        grid_spec=grid_spec,
        out_shape=[jax.ShapeDtypeStruct((b, d, t), BF16), jax.ShapeDtypeStruct((n, d), F32)],
        compiler_params=_params(3),
        name="attn",
    )(page_table.reshape(-1), bias, qt, k, vtb, tri, q, k_new, v_new, bias_rows, tris,
      *([cache_kt] * pages_per_step), *([cache_vt] * pages_per_step))


def _merge_body(x_ref, sa_ref, o_ref, cp_ref, pre_ref, post_ref, win_ref, wa_ref, wb_ref, wc_ref,
                wo_ref, y_ref, *, o_feature_major):
    x = x_ref[...]
    d = x.shape[-1]
    o_g = _proj_columns(sa_ref.shape[-1], cp_ref.shape[-1])[-1]
    wgate_ref = win_ref.at[:, o_g:o_g + 3 * d]
    u = _rms(x, pre_ref[...]).astype(BF16)
    if o_feature_major:
        branch_b = _dot_tn(o_ref[...], wb_ref[...])
    else:
        branch_b = _dot(o_ref[...].astype(BF16), wb_ref[...])
    merged = jax.nn.sigmoid(_dot(u, wgate_ref[:, 0:d])) * _dot(sa_ref[...], wa_ref[...])
    merged = merged + jax.nn.sigmoid(_dot(u, wgate_ref[:, d:2 * d])) * branch_b
    merged = merged + jax.nn.sigmoid(_dot(u, wgate_ref[:, 2 * d:3 * d])) * _dot(cp_ref[...], wc_ref[...])
    m = _dot(merged.astype(BF16), wo_ref[...])
    y_ref[...] = x + _rms(m, post_ref[...])


def _merge(x, sa, o, cp, pre_g, post_g, w_in, w_a, w_b, w_c, w_o, layer, tile, o_feature_major):
    b, t, d = x.shape
    rows = lambda w: pl.BlockSpec((None, tile, w), lambda bi, i: (bi, i, 0))
    o_spec = (pl.BlockSpec((None, o.shape[1], tile), lambda bi, i: (bi, 0, i)) if o_feature_major
              else rows(o.shape[2]))
    full = lambda a: _resident(a.shape)
    return pl.pallas_call(
        functools.partial(_merge_body, o_feature_major=o_feature_major),
        grid=(b, t // tile),
        in_specs=[rows(d), rows(sa.shape[2]), o_spec, rows(cp.shape[2]),
                  full(pre_g), full(post_g)] + [_layer_of(w, layer) for w in (w_in, w_a, w_b, w_c, w_o)],
        out_specs=rows(d),
        out_shape=jax.ShapeDtypeStruct((b, t, d), F32),
        compiler_params=_params(2),
        name="merge",
    )(x, sa, o, cp, pre_g, post_g, w_in, w_a, w_b, w_c, w_o)


ROW_TILE = 512
MIX_TILE = 256
ATTN_TILE = 256
ATTN_HEADS = 4


def _pad_rows_front(a, multiple):
    r = a.shape[-2]
    pad = -r % multiple
    return jnp.pad(a, [(0, 0)] * (a.ndim - 2) + [(pad, 0), (0, 0)])


def kernel(x_prompt, x_sample, cache_k, cache_v, state_conv_a, state_conv_c, page_table, ffn1_pre_g, ffn1_post_g, ffn1_w_gate, ffn1_w_up, ffn1_w_down, mix_pre_g, mix_post_g, w_in, conv_a_w, conv_a_b, ln_a_g, ln_a_b, w_a_out, w_b_out, sb_bias, conv_c_w, w_c_out, w_o, ffn2_pre_g, ffn2_post_g, ffn2_w_gate, ffn2_w_up, ffn2_w_down):
    b, t, d = x_prompt.shape
    n_seq, t_seq, _ = x_sample.shape
    depth, n_pool, page = cache_k.shape[:3]
    wa, d_a = conv_a_w.shape[1:]
    wc, d_c = conv_c_w.shape[1:]
    n = n_seq * t_seq
    o_q, o_k, o_v, o_c, o_g = _proj_columns(d_a, d_c)

    xp = x_prompt.reshape(b * t, d)
    xs = x_sample.reshape(n, d)
    ckt = jnp.transpose(cache_k, (0, 1, 3, 4, 2)).reshape(depth, n_pool, D_ATTN, page)
    cvt = jnp.transpose(cache_v, (0, 1, 3, 4, 2)).reshape(depth, n_pool, D_ATTN, page)
    zero_a = jnp.zeros((b, -(-(wa - 1) // SUBLANES) * SUBLANES, d_a), F32)
    zero_c = jnp.zeros((b, -(-(wc - 1) // SUBLANES) * SUBLANES, d_c), F32)
    sta_pad = _pad_rows_front(state_conv_a, SUBLANES)
    stc_pad = _pad_rows_front(state_conv_c, SUBLANES)
    vec = lambda a: a.reshape(1, -1)
    f1_w = [w.astype(BF16) for w in (ffn1_w_gate, ffn1_w_up, ffn1_w_down)]
    f2_w = [w.astype(BF16) for w in (ffn2_w_gate, ffn2_w_up, ffn2_w_down)]
    w_in_b = w_in.astype(BF16)
    out_w = [w.astype(BF16) for w in (w_a_out, w_b_out, w_c_out, w_o)]

    kv_stacks = None
    outs = [[] for _ in range(6)]
    for l in range(depth):
        f1 = (vec(ffn1_pre_g[l]), vec(ffn1_post_g[l]), *f1_w, l)
        f2 = (vec(ffn2_pre_g[l]), vec(ffn2_post_g[l]), *f2_w, l)
        proj_w = (vec(mix_pre_g[l]), w_in_b, w_in_b[l, :, o_q:o_k].T, w_in_b[l, :, o_v:o_c].T)
        conv = (conv_a_w[l], vec(conv_a_b[l]), vec(ln_a_g[l]), vec(ln_a_b[l]), conv_c_w[l])
        merge_w = (vec(mix_pre_g[l]), vec(mix_post_g[l]), w_in_b, *out_w, l)
        bias_rows = jnp.broadcast_to(jnp.repeat(sb_bias[l], t_seq)[:, None], (N_HEADS * t_seq, LANES))

        xp = _ffn(xp, *f1, tile=ROW_TILE)
        xs = _ffn(xs, *f1, tile=n)
        qt, kb, kt_all, vt_all, vtb, sa_p, cp_p, na_p, nc_p = _mixproj_prompt(
            xp.reshape(b, t, d), *proj_w, zero_a, zero_c, *conv, kv_stacks, l, depth, tile=MIX_TILE)
        kv_stacks = (kt_all, vt_all)
        q, k, v, sa_s, cp_s, na_s, nc_s = _mixproj_sample(xs, *proj_w, sta_pad[l], stc_pad[l], *conv, l,
                                                          n_seq=n_seq, t_seq=t_seq)
        ot, o = _attn(sb_bias[l], qt, kb, vtb, page_table, q, k, v, bias_rows, ckt, cvt, l, t_seq,
                      tile=ATTN_TILE, heads=ATTN_HEADS)
        xp = _merge(xp.reshape(b, t, d), sa_p, ot, cp_p, *merge_w, tile=ROW_TILE, o_feature_major=True)
        xs = _merge(xs[None], sa_s[None], o[None], cp_s[None], *merge_w, tile=n, o_feature_major=False)[0]
        xp = _ffn(xp.reshape(b * t, d), *f2, tile=ROW_TILE)
        xs = _ffn(xs, *f2, tile=n)
        for lst, val in zip(outs, (na_p, nc_p, k.reshape(n_seq, t_seq, N_HEADS, HEAD_DIM),
                                   v.reshape(n_seq, t_seq, N_HEADS, HEAD_DIM), na_s, nc_s)):
            lst.append(val)

    kv_out = lambda a: jnp.transpose(a.reshape(depth, b, N_HEADS, HEAD_DIM, t), (0, 1, 4, 2, 3))
    return (xp.reshape(b, t, d), xs.reshape(n_seq, t_seq, d), kv_out(kv_stacks[0]), kv_out(kv_stacks[1]),
            *[jnp.stack(o) for o in outs])
```
